```python
import math
import jax, jax.numpy as jnp
from jax import lax
import numpy as np

D_MODEL = 2048
BATCH = 2
SEQ = 8192
DEPTH = 4

MIX_WIDTH = D_MODEL
N_MIXERS = 4
GROUP_WIDTH = MIX_WIDTH // N_MIXERS
HEAD_DIM = 128
N_HEADS = GROUP_WIDTH // HEAD_DIM
LRU_BLOCKS = 8
LRU_BLOCK_DIM = GROUP_WIDTH // LRU_BLOCKS
LRU_C = 8.0
SHORT_CONV = 4
FFN_CONV = 3
D_FF = ((8 * D_MODEL // 3 + 255) // 256) * 256
FOX_BLOCK = 128
GDN_CHUNK = 64
DILATED_PAIRS = ((128, 1), (512, 4), (2048, 16))
EPS = 1e-6
NEG_INF = -1e30

IN_SIZES = (GROUP_WIDTH, GROUP_WIDTH,
            3 * GROUP_WIDTH, N_HEADS,
            3 * GROUP_WIDTH, GROUP_WIDTH, N_HEADS, N_HEADS,
            3 * GROUP_WIDTH)
IN_COLS = sum(IN_SIZES)

kernel_name = "hybrid_parallel_heads_rglru_fox_gdn_dilated"


def rmsnorm(x, gain):
    xf = x.astype(jnp.float32)
    y = xf * lax.rsqrt(jnp.mean(xf * xf, axis=-1, keepdims=True) + EPS)
    return (y * gain).astype(x.dtype)


def group_rmsnorm(y, gain, group):
    B, T, W = y.shape
    yg = y.astype(jnp.float32).reshape(B, T, W // group, group)
    yg = yg * lax.rsqrt(jnp.mean(yg * yg, axis=-1, keepdims=True) + EPS)
    return yg.reshape(B, T, W) * gain


def causal_dwconv(x, w, b=None):
    K = w.shape[0]
    T = x.shape[1]
    xp = jnp.pad(x, ((0, 0), (K - 1, 0), (0, 0)))
    y = sum(xp[:, k:k + T, :] * w[k] for k in range(K))
    return y if b is None else y + b


def split_cols(z, sizes):
    idx = np.cumsum(sizes)[:-1].tolist()
    return jnp.split(z, idx, axis=-1)


def to_heads(z):
    B, T, W = z.shape
    return z.reshape(B, T, W // HEAD_DIM, HEAD_DIM).transpose(0, 2, 1, 3)


def from_heads(z):
    B, H, T, d = z.shape
    return z.transpose(0, 2, 1, 3).reshape(B, T, H * d)


def l2norm(t):
    return t * lax.rsqrt(jnp.sum(t * t, axis=-1, keepdims=True) + EPS)


def rg_lru(xa, conv_w, conv_b, wa, ba, wx, bx, lam):
    B, T, W = xa.shape
    xc = causal_dwconv(xa, conv_w, conv_b).astype(jnp.float32)
    xb = xc.reshape(B, T, LRU_BLOCKS, LRU_BLOCK_DIM)
    r = jax.nn.sigmoid(jnp.einsum('btnc,ncd->btnd', xb, wa).reshape(B, T, W) + ba)
    i = jax.nn.sigmoid(jnp.einsum('btnc,ncd->btnd', xb, wx).reshape(B, T, W) + bx)
    log_a = -LRU_C * r * jax.nn.softplus(-lam)
    a = jnp.exp(log_a)
    u = jnp.sqrt(-jnp.expm1(2.0 * log_a)) * (i * xc)

    def combine(left, right):
        a_l, h_l = left
        a_r, h_r = right
        return a_l * a_r, a_r * h_l + h_r

    _, h = lax.associative_scan(combine, (a, u), axis=1)
    return h


def forgetting_attention(q, k, v, log_f):
    B, H, T, hd = q.shape
    nb = T // FOX_BLOCK
    c = jnp.cumsum(log_f, axis=-1)
    q = q * hd ** -0.5
    qb = jnp.moveaxis(q.reshape(B, H, nb, FOX_BLOCK, hd), 2, 0)
    cb = jnp.moveaxis(c.reshape(B, H, nb, FOX_BLOCK), 2, 0)
    starts = jnp.arange(nb) * FOX_BLOCK
    kpos = jnp.arange(T)

    def block(args):
        q_blk, c_blk, start = args
        s = jnp.einsum('bhqd,bhkd->bhqk', q_blk, k).astype(jnp.float32)
        s = s + c_blk[..., :, None] - c[..., None, :]
        qpos = start + jnp.arange(FOX_BLOCK)
        s = jnp.where(kpos[None, :] <= qpos[:, None], s, NEG_INF)
        p = jax.nn.softmax(s, axis=-1)
        return jnp.einsum('bhqk,bhkd->bhqd', p.astype(v.dtype), v)

    o = lax.map(block, (qb, cb, starts))
    return jnp.moveaxis(o, 0, 2).reshape(B, H, T, hd)


def gated_delta_rule(q, k, v, g, beta):
    B, H, T, dk = q.shape
    dv = v.shape[-1]
    C = GDN_CHUNK
    N = T // C
    q = q * dk ** -0.5
    qc = q.reshape(B, H, N, C, dk)
    kc = k.reshape(B, H, N, C, dk)
    vc = v.reshape(B, H, N, C, dv)
    bc = beta.reshape(B, H, N, C)
    gc = jnp.cumsum(g.reshape(B, H, N, C), axis=-1)
    tril = jnp.tril(jnp.ones((C, C), bool))
    strict = jnp.tril(jnp.ones((C, C), bool), -1)
    diff = gc[..., :, None] - gc[..., None, :]
    decay = jnp.where(tril, jnp.exp(jnp.where(tril, diff, 0.0)), 0.0)
    kbeta = kc * bc[..., None]
    vbeta = vc * bc[..., None]
    kk = jnp.einsum('bhnid,bhnjd->bhnij', kbeta, kc) * decay
    a_mat = jnp.where(strict, kk, 0.0) + jnp.eye(C, dtype=jnp.float32)
    rhs = jnp.concatenate([vbeta, kbeta * jnp.exp(gc)[..., None]], axis=-1)
    sol = lax.linalg.triangular_solve(a_mat, rhs, left_side=True, lower=True, unit_diagonal=True)
    u, w = sol[..., :dv], sol[..., dv:]
    qk = jnp.where(tril, jnp.einsum('bhnid,bhnjd->bhnij', qc, kc) * decay, 0.0)
    xs = tuple(jnp.moveaxis(t, 2, 0) for t in (qc, kc, u, w, qk, gc))

    def step(S, inp):
        q_i, k_i, u_i, w_i, qk_i, g_i = inp
        v_new = u_i - jnp.einsum('bhcd,bhde->bhce', w_i, S)
        o_inter = jnp.einsum('bhcd,bhde->bhce', q_i * jnp.exp(g_i)[..., None], S)
        o = o_inter + jnp.einsum('bhij,bhje->bhie', qk_i, v_new)
        g_last = g_i[..., -1:]
        S = S * jnp.exp(g_last)[..., None] + jnp.einsum(
            'bhcd,bhce->bhde', k_i * jnp.exp(g_last - g_i)[..., None], v_new)
        return S, o

    S0 = jnp.zeros((B, H, dk, dv), jnp.float32)
    _, o = lax.scan(step, S0, xs)
    return jnp.moveaxis(o, 0, 2).reshape(B, H, T, dv)


def gated_deltanet(qkv, z, beta_logit, alpha_logit, conv_w, a_log, dt_bias, norm_g):
    qkv = jax.nn.silu(causal_dwconv(qkv, conv_w)).astype(jnp.float32)
    q, k, v = [to_heads(t) for t in jnp.split(qkv, 3, axis=-1)]
    q, k = l2norm(q), l2norm(k)
    beta = jax.nn.sigmoid(beta_logit.astype(jnp.float32)).transpose(0, 2, 1)
    g = (-jnp.exp(a_log) * jax.nn.softplus(alpha_logit.astype(jnp.float32) + dt_bias)).transpose(0, 2, 1)
    o = gated_delta_rule(q, k, v, g, beta)
    o = o * lax.rsqrt(jnp.mean(o * o, axis=-1, keepdims=True) + EPS) * norm_g
    o = o * jax.nn.silu(to_heads(z).astype(jnp.float32))
    return from_heads(o)


def window_attention_lse(q, k, v, span):
    *lead, L, hd = q.shape
    nl = len(lead)
    n = -(-L // span)
    padw = [(0, 0)] * nl + [(0, n * span - L), (0, 0)]

    def blocks(t):
        return jnp.pad(t, padw).reshape(*lead, n, span, hd)

    def with_prev(t):
        prev = jnp.pad(t, [(0, 0)] * nl + [(1, 0), (0, 0), (0, 0)])[..., :-1, :, :]
        return jnp.concatenate([prev, t], axis=-2)

    qb = blocks(q * hd ** -0.5)
    kk = with_prev(blocks(k))
    vv = with_prev(blocks(v))
    s = jnp.einsum('...nqd,...nkd->...nqk', qb, kk).astype(jnp.float32)
    i = jnp.arange(span)[:, None]
    j = jnp.arange(2 * span)[None, :]
    dist = i + span - j
    key_pos = jnp.arange(n)[:, None, None] * span - span + j
    mask = (dist >= 0) & (dist <= span) & (key_pos >= 0)
    s = jnp.where(mask, s, NEG_INF)
    m = jnp.max(s, axis=-1, keepdims=True)
    p = jnp.exp(s - m)
    den = jnp.sum(p, axis=-1, keepdims=True)
    out = jnp.einsum('...nqk,...nkd->...nqd', p.astype(v.dtype), vv) / den.astype(v.dtype)
    lse = (m + jnp.log(den))[..., 0]
    out = out.reshape(*lead, n * span, hd)[..., :L, :]
    lse = lse.reshape(*lead, n * span)[..., :L]
    return out, lse


def dilated_branch(q, k, v, window, dil):
    B, H, T, hd = q.shape
    Td = T // dil

    def to_res(t):
        return t.reshape(B, H, Td, dil, hd).swapaxes(2, 3)

    o, lse = window_attention_lse(to_res(q), to_res(k), to_res(v), window // dil)
    return o.swapaxes(2, 3).reshape(B, H, T, hd), lse.swapaxes(2, 3).reshape(B, H, T)


def dilated_attention(q, k, v):
    outs, lses = [], []
    for window, dil in DILATED_PAIRS:
        o, lse = dilated_branch(q, k, v, window, dil)
        outs.append(o)
        lses.append(lse)
    wts = jax.nn.softmax(jnp.stack(lses), axis=0)
    return jnp.einsum('gbht,gbhtd->bhtd', wts.astype(q.dtype), jnp.stack(outs))


def conv_ffn(h, w_up, conv_w, conv_b, w_down):
    u = causal_dwconv(h @ w_up, conv_w, conv_b)
    up, gate = jnp.split(u, 2, axis=-1)
    return (jax.nn.silu(gate) * up) @ w_down


def setup_inputs(seed: int = 0) -> dict:
    key = jax.random.key(seed)
    ks = iter(jax.random.split(key, 32))
    f32 = jnp.float32

    def nrm(shape, scale):
        return scale * jax.random.normal(next(ks), shape, f32)

    def gain(shape):
        return 1.0 + 0.02 * jax.random.normal(next(ks), shape, f32)

    res_scale = (2 * DEPTH) ** -0.5
    x = jax.random.normal(next(ks), (BATCH, SEQ, D_MODEL), f32)
    norm_mix = gain((DEPTH, D_MODEL))
    w_in = nrm((DEPTH, D_MODEL, IN_COLS), D_MODEL ** -0.5)
    lru_conv_w = nrm((DEPTH, SHORT_CONV, GROUP_WIDTH), SHORT_CONV ** -0.5)
    lru_conv_b = nrm((DEPTH, GROUP_WIDTH), 0.01)
    lru_wa = nrm((DEPTH, LRU_BLOCKS, LRU_BLOCK_DIM, LRU_BLOCK_DIM), LRU_BLOCK_DIM ** -0.5)
    lru_ba = nrm((DEPTH, GROUP_WIDTH), 0.01)
    lru_wx = nrm((DEPTH, LRU_BLOCKS, LRU_BLOCK_DIM, LRU_BLOCK_DIM), LRU_BLOCK_DIM ** -0.5)
    lru_bx = nrm((DEPTH, GROUP_WIDTH), 0.01)
    a_c = jax.random.uniform(next(ks), (DEPTH, GROUP_WIDTH), f32, 0.9, 0.999)
    a_base = a_c ** (1.0 / LRU_C)
    lru_lambda = jnp.log(a_base) - jnp.log1p(-a_base)
    fox_f_bias = 3.0 + nrm((DEPTH, N_HEADS), 0.5)
    gdn_conv_w = nrm((DEPTH, SHORT_CONV, 3 * GROUP_WIDTH), SHORT_CONV ** -0.5)
    gdn_a_log = jnp.log(jax.random.uniform(next(ks), (DEPTH, N_HEADS), f32, 1.0, 16.0))
    dt = jnp.exp(jax.random.uniform(next(ks), (DEPTH, N_HEADS), f32, math.log(1e-3), math.log(1e-1)))
    gdn_dt_bias = dt + jnp.log(-jnp.expm1(-dt))
    gdn_norm = gain((DEPTH, HEAD_DIM))
    norm_a = gain((DEPTH, GROUP_WIDTH))
    norm_b = gain((DEPTH, GROUP_WIDTH))
    norm_d = gain((DEPTH, GROUP_WIDTH))
    w_out = nrm((DEPTH, MIX_WIDTH, D_MODEL), MIX_WIDTH ** -0.5 * res_scale)
    norm_ffn = gain((DEPTH, D_MODEL))
    ffn_w_up = nrm((DEPTH, D_MODEL, 2 * D_FF), D_MODEL ** -0.5)
    ffn_conv_w = nrm((DEPTH, FFN_CONV, 2 * D_FF), FFN_CONV ** -0.5)
    ffn_conv_b = nrm((DEPTH, 2 * D_FF), 0.01)
    ffn_w_down = nrm((DEPTH, D_FF, D_MODEL), D_FF ** -0.5 * res_scale)
    norm_final = gain((D_MODEL,))
    return {"x": x, "norm_mix": norm_mix, "w_in": w_in,
            "lru_conv_w": lru_conv_w, "lru_conv_b": lru_conv_b,
            "lru_wa": lru_wa, "lru_ba": lru_ba, "lru_wx": lru_wx, "lru_bx": lru_bx,
            "lru_lambda": lru_lambda, "fox_f_bias": fox_f_bias,
            "gdn_conv_w": gdn_conv_w, "gdn_a_log": gdn_a_log, "gdn_dt_bias": gdn_dt_bias,
            "gdn_norm": gdn_norm, "norm_a": norm_a, "norm_b": norm_b, "norm_d": norm_d,
            "w_out": w_out, "norm_ffn": norm_ffn, "ffn_w_up": ffn_w_up,
            "ffn_conv_w": ffn_conv_w, "ffn_conv_b": ffn_conv_b, "ffn_w_down": ffn_w_down,
            "norm_final": norm_final}


def reference(x, norm_mix, w_in, lru_conv_w, lru_conv_b, lru_wa, lru_ba, lru_wx, lru_bx,
              lru_lambda, fox_f_bias, gdn_conv_w, gdn_a_log, gdn_dt_bias, gdn_norm,
              norm_a, norm_b, norm_d, w_out, norm_ffn, ffn_w_up, ffn_conv_w, ffn_conv_b,
              ffn_w_down, norm_final):
    for l in range(DEPTH):
        h = rmsnorm(x, norm_mix[l])
        z = h @ w_in[l]
        a_x, a_gate, b_qkv, b_f, c_qkv, c_z, c_beta, c_alpha, d_qkv = split_cols(z, IN_SIZES)

        h_a = rg_lru(a_x, lru_conv_w[l], lru_conv_b[l], lru_wa[l], lru_ba[l],
                     lru_wx[l], lru_bx[l], lru_lambda[l])
        y_a = group_rmsnorm(h_a, norm_a[l], LRU_BLOCK_DIM) * jax.nn.gelu(a_gate.astype(jnp.float32))

        bq, bk, bv = [to_heads(t) for t in jnp.split(b_qkv, 3, axis=-1)]
        log_f = jax.nn.log_sigmoid(b_f.astype(jnp.float32) + fox_f_bias[l]).transpose(0, 2, 1)
        y_b = group_rmsnorm(from_heads(forgetting_attention(bq, bk, bv, log_f)), norm_b[l], HEAD_DIM)

        y_c = gated_deltanet(c_qkv, c_z, c_beta, c_alpha, gdn_conv_w[l], gdn_a_log[l],
                             gdn_dt_bias[l], gdn_norm[l])

        dq, dk, dv = [to_heads(t) for t in jnp.split(d_qkv, 3, axis=-1)]
        y_d = group_rmsnorm(from_heads(dilated_attention(dq, dk, dv)), norm_d[l], HEAD_DIM)

        y = jnp.concatenate([y_a, y_b, y_c, y_d], axis=-1).astype(x.dtype)
        x = x + y @ w_out[l]

        h = rmsnorm(x, norm_ffn[l])
        x = x + conv_ffn(h, ffn_w_up[l], ffn_conv_w[l], ffn_conv_b[l], ffn_w_down[l])
    return rmsnorm(x, norm_final)
```

```python
import functools
import math

import jax
import jax.numpy as jnp
from jax import lax
from jax.experimental import pallas as pl
from jax.experimental.pallas import tpu as pltpu

F32 = jnp.float32
BF16 = jnp.bfloat16

D_MODEL = 2048
GROUP_WIDTH = 512
HEAD_DIM = 128
N_HEADS = 4
LRU_BLOCKS = 8
LRU_BLOCK_DIM = 64
LRU_C = 8.0
SHORT_CONV = 4
FFN_CONV = 3
D_FF = 5632
GDN_CHUNK = 64
SPAN = 128
DILATIONS = (1, 4, 16)
EPS = 1e-6
NEG_INF = -1e30
ATTN_SCALE = HEAD_DIM ** -0.5

LANE = 128
SUBLANE = 8
VMEM_LIMIT = 52 * 1024 * 1024

COL_CQKV = 0
COL_AX = 1536
COL_AG = 2048
COL_CZ = 2560
COL_BQKV = 3072
COL_DQKV = 4608
COL_SMALL = 6144
Z_COLS = 6272
SMALL_F, SMALL_BETA, SMALL_ALPHA = 0, 4, 8


def _params(*sem):
    return pltpu.CompilerParams(dimension_semantics=sem, vmem_limit_bytes=VMEM_LIMIT)


def _sigmoid(x):
    return 1.0 / (1.0 + jnp.exp(-x))


def _softplus(x):
    return jnp.maximum(x, 0.0) + jnp.log1p(jnp.exp(-jnp.abs(x)))


def _silu(x):
    return x * _sigmoid(x)


def _gelu_tanh(x):
    c = math.sqrt(2.0 / math.pi)
    return 0.5 * x * (1.0 + jnp.tanh(c * (x + 0.044715 * (x * x * x))))


def _dot(a, b, precision=None):
    return jnp.dot(a, b, preferred_element_type=F32, precision=precision)


def _dot_nt(a, b):
    return lax.dot_general(a, b, (((1,), (1,)), ((), ())), preferred_element_type=F32)


def _dot_tn(a, b):
    return lax.dot_general(a, b, (((0,), (0,)), ((), ())), preferred_element_type=F32)


def _causal_conv(x, halo, w, taps):
    n = x.shape[0]
    cat = jnp.concatenate([halo, x], axis=0)
    y = w[taps - 1:taps, :] * x
    for k in range(taps - 1):
        shifted = pltpu.roll(cat, taps - 1 - k, 0)[SUBLANE:SUBLANE + n, :]
        y = y + w[k:k + 1, :] * shifted
    return y


def _scan_affine(a, u):
    n = a.shape[0]
    row = lax.broadcasted_iota(jnp.int32, a.shape, 0)
    s = 1
    while s < n:
        keep = row >= s
        a_s = jnp.where(keep, pltpu.roll(a, s, 0), 1.0)
        u_s = jnp.where(keep, pltpu.roll(u, s, 0), 0.0)
        u = a * u_s + u
        a = a * a_s
        s *= 2
    return a, u


def _scan_add(x, seg):
    row = lax.broadcasted_iota(jnp.int32, x.shape, 0)
    pos = row % seg
    s = 1
    while s < seg:
        x = x + jnp.where(pos >= s, pltpu.roll(x, s, 0), 0.0)
        s *= 2
    return x


def _rmsnorm_rows(x, gain):
    ms = jnp.mean(x * x, axis=-1, keepdims=True)
    return x * lax.rsqrt(ms + EPS) * gain


def _rmsnorm_kernel(x_ref, g_ref, o_ref):
    o_ref[...] = _rmsnorm_rows(x_ref[...], g_ref[...]).astype(o_ref.dtype)


def rmsnorm_call(x, gain, out_dtype, tm=512):
    m, d = x.shape
    return pl.pallas_call(
        _rmsnorm_kernel,
        grid=(m // tm,),
        in_specs=[pl.BlockSpec((tm, d), lambda i: (i, 0)),
                  pl.BlockSpec((1, d), lambda i: (0, 0))],
        out_specs=pl.BlockSpec((tm, d), lambda i: (i, 0)),
        out_shape=jax.ShapeDtypeStruct((m, d), out_dtype),
        compiler_params=_params("parallel"),
        name="rmsnorm",
    )(x, gain.reshape(1, d))


def _matmul_kernel(a_ref, b_ref, o_ref):
    o_ref[...] = _dot(a_ref[...], b_ref[...])


def in_proj_call(h, w, tm=1024, tn=896):
    m, k = h.shape
    n = w.shape[1]
    tm = min(tm, m)
    return pl.pallas_call(
        _matmul_kernel,
        grid=(m // tm, n // tn),
        in_specs=[pl.BlockSpec((tm, k), lambda i, j: (i, 0)),
                  pl.BlockSpec((k, tn), lambda i, j: (0, j))],
        out_specs=pl.BlockSpec((tm, tn), lambda i, j: (i, j)),
        out_shape=jax.ShapeDtypeStruct((m, n), F32),
        compiler_params=_params("parallel", "arbitrary"),
        name="in_proj",
    )(h, w)


def _group_mean_sq(h, gmat):
    h2 = h * h
    hi = h2.astype(BF16)
    lo = (h2 - hi.astype(F32)).astype(BF16)
    return _dot(hi, gmat) + _dot(lo, gmat)


def _lru_kernel(ax_ref, ag_ref, cw_ref, cb_ref, wa_ref, ba_ref, wx_ref, bx_ref, lam_ref,
                gmat_ref, gain_ref, o_ref, halo_ref, hprev_ref):
    tt = ax_ref.shape[0]

    @pl.when(pl.program_id(1) == 0)
    def _():
        halo_ref[...] = jnp.zeros_like(halo_ref)
        hprev_ref[...] = jnp.zeros_like(hprev_ref)

    x = ax_ref[...]
    xc = _causal_conv(x, halo_ref[...], cw_ref[...], SHORT_CONV) + cb_ref[...]
    halo_ref[...] = x[tt - SUBLANE:, :]
    xb = xc.astype(BF16)
    r = _sigmoid(_dot(xb, wa_ref[...]) + ba_ref[...])
    i = _sigmoid(_dot(xb, wx_ref[...]) + bx_ref[...])
    log_a = (-LRU_C) * r * _softplus(-lam_ref[...])
    a = jnp.exp(log_a)
    u = jnp.sqrt(-jnp.tanh(log_a) * (a * a + 1.0)) * (i * xc)
    a_cum, h = _scan_affine(a, u)
    h = h + a_cum * hprev_ref[...]
    hprev_ref[...] = h[tt - 1:tt, :]
    y = h * lax.rsqrt(_group_mean_sq(h, gmat_ref[...]) + EPS) * gain_ref[...]
    o_ref[...] = (y * _gelu_tanh(ag_ref[...])).astype(o_ref.dtype)


def lru_call(z, p, batch, seq, tt=512):
    tt = min(tt, seq)
    nt = seq // tt
    w = GROUP_WIDTH
    row = lambda b, t: (b * nt + t)
    vec = pl.BlockSpec((1, w), lambda b, t: (0, 0))
    mat = pl.BlockSpec((w, w), lambda b, t: (0, 0))
    return pl.pallas_call(
        _lru_kernel,
        grid=(batch, nt),
        in_specs=[pl.BlockSpec((tt, w), lambda b, t: (row(b, t), COL_AX // w)),
                  pl.BlockSpec((tt, w), lambda b, t: (row(b, t), COL_AG // w)),
                  pl.BlockSpec((SHORT_CONV, w), lambda b, t: (0, 0)),
                  vec, mat, vec, mat, vec, vec, mat, vec],
        out_specs=pl.BlockSpec((tt, w), lambda b, t: (row(b, t), 0)),
        out_shape=jax.ShapeDtypeStruct((batch * seq, w), BF16),
        scratch_shapes=[pltpu.VMEM((SUBLANE, w), F32), pltpu.VMEM((1, w), F32)],
        compiler_params=_params("parallel", "arbitrary"),
        name="rg_lru",
    )(z, z, p["lru_conv_w"], p["lru_conv_b"], p["lru_wa"], p["lru_ba"], p["lru_wx"],
      p["lru_bx"], p["lru_lambda"], p["gmat64"], p["norm_a"])


def _logf_cumsum_kernel(s_ref, bias_ref, o_ref, carry_ref):
    tt = s_ref.shape[0]

    @pl.when(pl.program_id(1) == 0)
    def _():
        carry_ref[...] = jnp.zeros_like(carry_ref)

    log_f = -_softplus(-(s_ref[...] + bias_ref[...]))
    c = _scan_add(log_f, tt) + carry_ref[...]
    carry_ref[...] = c[tt - 1:tt, :]
    o_ref[...] = c


def logf_cumsum_call(z, bias, batch, seq, tt=1024):
    tt = min(tt, seq)
    nt = seq // tt
    return pl.pallas_call(
        _logf_cumsum_kernel,
        grid=(batch, nt),
        in_specs=[pl.BlockSpec((tt, LANE), lambda b, t: (b * nt + t, COL_SMALL // LANE)),
                  pl.BlockSpec((1, LANE), lambda b, t: (0, 0))],
        out_specs=pl.BlockSpec((tt, LANE), lambda b, t: (b * nt + t, 0)),
        out_shape=jax.ShapeDtypeStruct((batch * seq, LANE), F32),
        scratch_shapes=[pltpu.VMEM((1, LANE), F32)],
        compiler_params=_params("parallel", "arbitrary"),
        name="logf_cumsum",
    )(z, bias)


def _fox_kernel(q_ref, k_ref, v_ref, cq_ref, ck_ref, gain_ref, o_ref, kb_ref, vb_ref, *, tq):
    qi = pl.program_id(2)

    @pl.when(qi == 0)
    def _():
        kb_ref[...] = k_ref[...].astype(BF16)
        vb_ref[...] = v_ref[...].astype(BF16)

    q = (q_ref[...] * ATTN_SCALE).astype(BF16)
    cq = cq_ref[...]

    def step(kv, carry, masked):
        m, l, acc = carry
        start = pl.multiple_of(kv * tq, tq)
        k = kb_ref[pl.ds(start, tq), :]
        v = vb_ref[pl.ds(start, tq), :]
        s = _dot_nt(q, k) + (cq - ck_ref[kv])
        if masked:
            row = lax.broadcasted_iota(jnp.int32, s.shape, 0)
            col = lax.broadcasted_iota(jnp.int32, s.shape, 1)
            s = jnp.where(col <= row, s, NEG_INF)
        m_new = jnp.maximum(m, jnp.max(s, axis=-1, keepdims=True))
        p = jnp.exp(s - m_new)
        alpha = jnp.exp(m - m_new)
        l = alpha * l + jnp.sum(p, axis=-1, keepdims=True)
        acc = alpha * acc + _dot(p.astype(BF16), v)
        return m_new, l, acc

    init = (jnp.full((tq, 1), NEG_INF, F32), jnp.zeros((tq, 1), F32),
            jnp.zeros((tq, HEAD_DIM), F32))
    carry = lax.fori_loop(0, qi, lambda kv, c: step(kv, c, False), init)
    _, l, acc = step(qi, carry, True)
    o = acc / l
    o = o * lax.rsqrt(jnp.mean(o * o, axis=-1, keepdims=True) + EPS) * gain_ref[...]
    o_ref[...] = o.astype(o_ref.dtype)


def fox_call(z3, c_col, c_row, gain, batch, seq, tq=512):
    tq = min(tq, seq)
    nq = seq // tq
    qb, kb, vb = (COL_BQKV // LANE, (COL_BQKV + GROUP_WIDTH) // LANE,
                  (COL_BQKV + 2 * GROUP_WIDTH) // LANE)
    return pl.pallas_call(
        functools.partial(_fox_kernel, tq=tq),
        grid=(batch, N_HEADS, nq),
        in_specs=[pl.BlockSpec((None, tq, LANE), lambda b, h, i: (b, i, qb + h)),
                  pl.BlockSpec((None, seq, LANE), lambda b, h, i: (b, 0, kb + h)),
                  pl.BlockSpec((None, seq, LANE), lambda b, h, i: (b, 0, vb + h)),
                  pl.BlockSpec((None, None, tq, 1), lambda b, h, i: (b, h, i, 0)),
                  pl.BlockSpec((None, None, nq, 1, tq), lambda b, h, i: (b, h, 0, 0, 0)),
                  pl.BlockSpec((1, LANE), lambda b, h, i: (0, h))],
        out_specs=pl.BlockSpec((None, tq, LANE), lambda b, h, i: (b, i, h)),
        out_shape=jax.ShapeDtypeStruct((batch, seq, GROUP_WIDTH), BF16),
        scratch_shapes=[pltpu.VMEM((seq, LANE), BF16), pltpu.VMEM((seq, LANE), BF16)],
        compiler_params=_params("parallel", "parallel", "arbitrary"),
        name="fox_attention",
    )(z3, z3, z3, c_col, c_row, gain)


def _unit_lower_inverse(strict_lower):
    c = strict_lower.shape[0]
    hi = lax.Precision.HIGHEST
    row = lax.broadcasted_iota(jnp.int32, (c, c), 0)
    col = lax.broadcasted_iota(jnp.int32, (c, c), 1)
    q = -strict_lower
    r = jnp.where(row == col, 1.0, q)
    cover = 2
    while cover < c:
        q = _dot(q, q, hi)
        r = r + _dot(r, q, hi)
        cover *= 2
    return r


def _l2norm(t):
    return t * lax.rsqrt(jnp.sum(t * t, axis=-1, keepdims=True) + EPS)


def _gdn_kernel(x_ref, zg_ref, s_ref, cw_ref, aneg_ref, dt_ref, gain_ref, o_ref,
                halo_ref, qkv_ref, beta_ref, gc_ref, state_ref):
    tt = x_ref.shape[0]
    c = GDN_CHUNK
    w = GROUP_WIDTH

    @pl.when(pl.program_id(1) == 0)
    def _():
        halo_ref[...] = jnp.zeros_like(halo_ref)
        state_ref[...] = jnp.zeros_like(state_ref)

    x = x_ref[...]
    qkv_ref[...] = _silu(_causal_conv(x, halo_ref[...], cw_ref[...], SHORT_CONV))
    halo_ref[...] = x[tt - SUBLANE:, :]
    small = s_ref[...]
    beta_ref[...] = _sigmoid(small)
    gc_ref[...] = _scan_add(aneg_ref[...] * _softplus(small + dt_ref[...]), c)

    row = lax.broadcasted_iota(jnp.int32, (c, c), 0)
    col = lax.broadcasted_iota(jnp.int32, (c, c), 1)
    tril = col <= row
    strict = col < row
    hi = lax.Precision.HIGHEST

    def chunk(ci, carry):
        r0 = pl.multiple_of(ci * c, c)
        rows = pl.ds(r0, c)
        beta_all = beta_ref[rows, :]
        gc_all = gc_ref[rows, :]
        gc_t = gc_all.T
        for h in range(N_HEADS):
            q = _l2norm(qkv_ref[rows, h * HEAD_DIM:(h + 1) * HEAD_DIM]) * ATTN_SCALE
            k = _l2norm(qkv_ref[rows, w + h * HEAD_DIM:w + (h + 1) * HEAD_DIM])
            v = qkv_ref[rows, 2 * w + h * HEAD_DIM:2 * w + (h + 1) * HEAD_DIM]
            beta = beta_all[:, SMALL_BETA + h:SMALL_BETA + h + 1]
            gc = gc_all[:, SMALL_ALPHA + h:SMALL_ALPHA + h + 1]
            gc_row = gc_t[SMALL_ALPHA + h:SMALL_ALPHA + h + 1, :]
            decay = jnp.where(tril, jnp.exp(jnp.where(tril, gc - gc_row, 0.0)), 0.0)
            kb = k.astype(BF16)
            kbeta = k * beta
            kk = _dot_nt(kbeta.astype(BF16), kb) * decay
            t_inv = _unit_lower_inverse(jnp.where(strict, kk, 0.0))
            rhs = jnp.concatenate([v * beta, kbeta * jnp.exp(gc)], axis=-1)
            sol = _dot(t_inv, rhs, hi)
            u = sol[:, :HEAD_DIM]
            wy = sol[:, HEAD_DIM:]
            qk = jnp.where(tril, _dot_nt(q.astype(BF16), kb) * decay, 0.0)
            state = state_ref[h]
            sb = state.astype(BF16)
            v_new = u - _dot(wy.astype(BF16), sb)
            vb = v_new.astype(BF16)
            o = _dot((q * jnp.exp(gc)).astype(BF16), sb) + _dot(qk.astype(BF16), vb)
            g_last = gc[c - 1:c, :]
            k_dec = (k * jnp.exp(g_last - gc)).astype(BF16)
            state_ref[h] = state * jnp.exp(g_last) + _dot_tn(k_dec, vb)
            o = o * lax.rsqrt(jnp.mean(o * o, axis=-1, keepdims=True) + EPS) * gain_ref[...]
            zg = zg_ref[rows, h * HEAD_DIM:(h + 1) * HEAD_DIM]
            o_ref[rows, h * HEAD_DIM:(h + 1) * HEAD_DIM] = (o * _silu(zg)).astype(o_ref.dtype)
        return carry

    lax.fori_loop(0, tt // c, chunk, 0)


def gdn_call(z, p, batch, seq, tt=256):
    tt = min(tt, seq)
    nt = seq // tt
    w = GROUP_WIDTH
    vec = pl.BlockSpec((1, LANE), lambda b, t: (0, 0))
    return pl.pallas_call(
        _gdn_kernel,
        grid=(batch, nt),
        in_specs=[pl.BlockSpec((tt, 3 * w), lambda b, t: (b * nt + t, COL_CQKV // (3 * w))),
                  pl.BlockSpec((tt, w), lambda b, t: (b * nt + t, COL_CZ // w)),
                  pl.BlockSpec((tt, LANE), lambda b, t: (b * nt + t, COL_SMALL // LANE)),
                  pl.BlockSpec((SHORT_CONV, 3 * w), lambda b, t: (0, 0)),
                  vec, vec, vec],
        out_specs=pl.BlockSpec((tt, w), lambda b, t: (b * nt + t, 0)),
        out_shape=jax.ShapeDtypeStruct((batch * seq, w), BF16),
        scratch_shapes=[pltpu.VMEM((SUBLANE, 3 * w), F32),
                        pltpu.VMEM((tt, 3 * w), F32),
                        pltpu.VMEM((tt, LANE), F32),
                        pltpu.VMEM((tt, LANE), F32),
                        pltpu.VMEM((N_HEADS, HEAD_DIM, HEAD_DIM), F32)],
        compiler_params=_params("parallel", "arbitrary"),
        name="gated_deltanet",
    )(z, z, z, p["gdn_conv_w"], p["gdn_aneg"], p["gdn_dt"], p["gdn_norm"])


def _band_kernel(q_ref, kc_ref, kp_ref, vc_ref, vp_ref, o_ref, lse_ref):
    tq = q_ref.shape[0]
    first = pl.program_id(3) == 0
    kcat = jnp.concatenate([kp_ref[...], kc_ref[...]], axis=0).astype(BF16)
    vcat = jnp.concatenate([vp_ref[...], vc_ref[...]], axis=0).astype(BF16)
    i = lax.broadcasted_iota(jnp.int32, (SPAN, 2 * SPAN), 0)
    j = lax.broadcasted_iota(jnp.int32, (SPAN, 2 * SPAN), 1)
    dist = i + SPAN - j
    band = (dist >= 0) & (dist <= SPAN)
    for sb in range(tq // SPAN):
        q = (q_ref[sb * SPAN:(sb + 1) * SPAN, :] * ATTN_SCALE).astype(BF16)
        k = kcat[sb * SPAN:(sb + 2) * SPAN, :]
        v = vcat[sb * SPAN:(sb + 2) * SPAN, :]
        s = _dot_nt(q, k)
        mask = band
        if sb == 0:
            mask = band & jnp.logical_or(j >= SPAN, jnp.logical_not(first))
        s = jnp.where(mask, s, NEG_INF)
        m = jnp.max(s, axis=-1, keepdims=True)
        p = jnp.exp(s - m)
        den = jnp.sum(p, axis=-1, keepdims=True)
        o_ref[sb * SPAN:(sb + 1) * SPAN, :] = _dot(p.astype(BF16), v) / den
        lse_ref[sb * SPAN:(sb + 1) * SPAN, :] = jnp.broadcast_to(m + jnp.log(den), (SPAN, LANE))


def band_call(z, batch, seq, dil, tq=512):
    td = seq // dil
    tq = min(tq, td)
    nq = td // tq
    zc = Z_COLS // LANE
    oc = GROUP_WIDTH // LANE
    zv = z.reshape(batch, td, dil * Z_COLS)
    qb, kb, vb = (COL_DQKV // LANE, (COL_DQKV + GROUP_WIDTH) // LANE,
                  (COL_DQKV + 2 * GROUP_WIDTH) // LANE)
    ratio = tq // SPAN

    def cur(off):
        return pl.BlockSpec((None, tq, LANE), lambda b, h, r, i: (b, i, r * zc + off + h))

    def prev(off):
        return pl.BlockSpec((None, SPAN, LANE),
                            lambda b, h, r, i: (b, jnp.maximum(i * ratio - 1, 0), r * zc + off + h))

    out_spec = pl.BlockSpec((None, tq, LANE), lambda b, h, r, i: (b, i, r * oc + h))
    shape = jax.ShapeDtypeStruct((batch, td, dil * GROUP_WIDTH), F32)
    o, lse = pl.pallas_call(
        _band_kernel,
        grid=(batch, N_HEADS, dil, nq),
        in_specs=[cur(qb), cur(kb), prev(kb), cur(vb), prev(vb)],
        out_specs=[out_spec, out_spec],
        out_shape=[shape, shape],
        compiler_params=_params("parallel", "parallel", "parallel", "arbitrary"),
        name="dilated_band_%d" % dil,
    )(zv, zv, zv, zv, zv)
    return (o.reshape(batch * seq, GROUP_WIDTH), lse.reshape(batch * seq, GROUP_WIDTH))


def _merge_kernel(o0, l0, o1, l1, o2, l2, gain_ref, y_ref):
    lses = (l0[...], l1[...], l2[...])
    m = jnp.maximum(jnp.maximum(lses[0], lses[1]), lses[2])
    es = [jnp.exp(l - m) for l in lses]
    den = es[0] + es[1] + es[2]
    y = (es[0] / den) * o0[...] + (es[1] / den) * o1[...] + (es[2] / den) * o2[...]
    for h in range(N_HEADS):
        yh = y[:, h * HEAD_DIM:(h + 1) * HEAD_DIM]
        yh = yh * lax.rsqrt(jnp.mean(yh * yh, axis=-1, keepdims=True) + EPS)
        y_ref[:, h * HEAD_DIM:(h + 1) * HEAD_DIM] = (
            yh * gain_ref[:, h * HEAD_DIM:(h + 1) * HEAD_DIM]).astype(y_ref.dtype)


def merge_call(branches, gain, tm=512):
    m, w = branches[0][0].shape
    tm = min(tm, m)
    blk = pl.BlockSpec((tm, w), lambda i: (i, 0))
    flat = [a for pair in branches for a in pair]
    return pl.pallas_call(
        _merge_kernel,
        grid=(m // tm,),
        in_specs=[blk] * 6 + [pl.BlockSpec((1, w), lambda i: (0, 0))],
        out_specs=blk,
        out_shape=jax.ShapeDtypeStruct((m, w), BF16),
        compiler_params=_params("parallel"),
        name="dilated_merge",
    )(*flat, gain)


def _out_proj_kernel(ya, yb, yc, yd, w_ref, x_ref, g_ref, x1_ref, h_ref):
    w = GROUP_WIDTH
    acc = x_ref[...]
    for n, y in enumerate((ya, yb, yc, yd)):
        acc = acc + _dot(y[...], w_ref[n * w:(n + 1) * w, :])
    x1_ref[...] = acc
    h_ref[...] = _rmsnorm_rows(acc, g_ref[...]).astype(h_ref.dtype)


def out_proj_call(ys, w_out, x, gain, tm=256):
    m, d = x.shape
    tm = min(tm, m)
    yblk = pl.BlockSpec((tm, GROUP_WIDTH), lambda i: (i, 0))
    xblk = pl.BlockSpec((tm, d), lambda i: (i, 0))
    return pl.pallas_call(
        _out_proj_kernel,
        grid=(m // tm,),
        in_specs=[yblk] * 4 + [pl.BlockSpec((d, d), lambda i: (0, 0)), xblk,
                               pl.BlockSpec((1, d), lambda i: (0, 0))],
        out_specs=[xblk, xblk],
        out_shape=[jax.ShapeDtypeStruct((m, d), F32), jax.ShapeDtypeStruct((m, d), BF16)],
        compiler_params=_params("parallel"),
        name="out_proj",
    )(*ys, w_out, x, gain)


def _ffn_up_kernel(h_ref, wu_ref, wg_ref, cwu_ref, cwg_ref, cbu_ref, cbg_ref, o_ref,
                   tail_u_ref, tail_g_ref, *, tiles_per_seq):
    tm = h_ref.shape[0]

    @pl.when(pl.program_id(1) % tiles_per_seq == 0)
    def _():
        tail_u_ref[...] = jnp.zeros_like(tail_u_ref)
        tail_g_ref[...] = jnp.zeros_like(tail_g_ref)

    h = h_ref[...]
    u = _dot(h, wu_ref[...])
    g = _dot(h, wg_ref[...])
    up = _causal_conv(u, tail_u_ref[...], cwu_ref[...], FFN_CONV) + cbu_ref[...]
    gate = _causal_conv(g, tail_g_ref[...], cwg_ref[...], FFN_CONV) + cbg_ref[...]
    tail_u_ref[...] = u[tm - SUBLANE:, :]
    tail_g_ref[...] = g[tm - SUBLANE:, :]
    o_ref[...] = (_silu(gate) * up).astype(o_ref.dtype)


def ffn_up_call(h, w_up, conv_w, conv_b, seq, tm=512, tn=512):
    m, d = h.shape
    tm = min(tm, seq)
    nj = D_FF // tn
    return pl.pallas_call(
        functools.partial(_ffn_up_kernel, tiles_per_seq=seq // tm),
        grid=(nj, m // tm),
        in_specs=[pl.BlockSpec((tm, d), lambda j, i: (i, 0)),
                  pl.BlockSpec((d, tn), lambda j, i: (0, j)),
                  pl.BlockSpec((d, tn), lambda j, i: (0, j + nj)),
                  pl.BlockSpec((FFN_CONV, tn), lambda j, i: (0, j)),
                  pl.BlockSpec((FFN_CONV, tn), lambda j, i: (0, j + nj)),
                  pl.BlockSpec((1, tn), lambda j, i: (0, j)),
                  pl.BlockSpec((1, tn), lambda j, i: (0, j + nj))],
        out_specs=pl.BlockSpec((tm, tn), lambda j, i: (i, j)),
        out_shape=jax.ShapeDtypeStruct((m, D_FF), BF16),
        scratch_shapes=[pltpu.VMEM((SUBLANE, tn), F32), pltpu.VMEM((SUBLANE, tn), F32)],
        compiler_params=_params("parallel", "arbitrary"),
        name="ffn_up",
    )(h, w_up, w_up, conv_w, conv_w, conv_b, conv_b)


def _ffn_down_kernel(g_ref, w_ref, x_ref, gain_ref, x2_ref, hn_ref, acc_ref):
    kk = pl.program_id(1)

    @pl.when(kk == 0)
    def _():
        acc_ref[...] = x_ref[...]

    acc_ref[...] += _dot(g_ref[...], w_ref[...])

    @pl.when(kk == pl.num_programs(1) - 1)
    def _():
        x2 = acc_ref[...]
        x2_ref[...] = x2
        hn_ref[...] = _rmsnorm_rows(x2, gain_ref[...]).astype(hn_ref.dtype)


def ffn_down_call(g, w_down, x, gain, hn_dtype, tm=512, tk=512):
    m, d = x.shape
    tm = min(tm, m)
    kdim = g.shape[1]
    xblk = pl.BlockSpec((tm, d), lambda i, k: (i, 0))
    return pl.pallas_call(
        _ffn_down_kernel,
        grid=(m // tm, kdim // tk),
        in_specs=[pl.BlockSpec((tm, tk), lambda i, k: (i, k)),
                  pl.BlockSpec((tk, d), lambda i, k: (k, 0)),
                  xblk, pl.BlockSpec((1, d), lambda i, k: (0, 0))],
        out_specs=[xblk, xblk],
        out_shape=[jax.ShapeDtypeStruct((m, d), F32), jax.ShapeDtypeStruct((m, d), hn_dtype)],
        scratch_shapes=[pltpu.VMEM((tm, d), F32)],
        compiler_params=_params("parallel", "arbitrary"),
        name="ffn_down",
    )(g, w_down, x, gain)


def _block_diag(blocks):
    n, r, c = blocks.shape
    eye = jnp.eye(n, dtype=blocks.dtype)
    return (eye[:, None, :, None] * blocks[:, :, None, :]).reshape(n * r, n * c)


def _small_vec(values, offset):
    return jnp.zeros((1, LANE), F32).at[0, offset:offset + values.shape[0]].set(values)


def _prep_w_in(w):
    gw = GROUP_WIDTH
    sizes = (gw, gw, 3 * gw, N_HEADS, 3 * gw, gw, N_HEADS, N_HEADS, 3 * gw)
    offs = [0]
    for s in sizes:
        offs.append(offs[-1] + s)
    a_x, a_g, b_qkv, b_f, c_qkv, c_z, c_beta, c_alpha, d_qkv = [
        w[:, offs[n]:offs[n + 1]] for n in range(len(sizes))]
    pad = jnp.zeros((w.shape[0], Z_COLS - COL_SMALL - 3 * N_HEADS), w.dtype)
    return jnp.concatenate([c_qkv, a_x, a_g, c_z, b_qkv, d_qkv, b_f, c_beta, c_alpha, pad],
                           axis=1).astype(BF16)


def kernel(x, norm_mix, w_in, lru_conv_w, lru_conv_b, lru_wa, lru_ba, lru_wx, lru_bx,
           lru_lambda, fox_f_bias, gdn_conv_w, gdn_a_log, gdn_dt_bias, gdn_norm,
           norm_a, norm_b, norm_d, w_out, norm_ffn, ffn_w_up, ffn_conv_w, ffn_conv_b,
           ffn_w_down, norm_final):
    batch, seq, d = x.shape
    depth = w_in.shape[0]
    m = batch * seq
    gw = GROUP_WIDTH
    gmat64 = _block_diag(jnp.full((LRU_BLOCKS, LRU_BLOCK_DIM, LRU_BLOCK_DIM),
                                  1.0 / LRU_BLOCK_DIM, F32)).astype(BF16)
    xs = x.reshape(m, d)
    h = rmsnorm_call(xs, norm_mix[0], BF16)
    for l in range(depth):
        p = {
            "lru_conv_w": lru_conv_w[l], "lru_conv_b": lru_conv_b[l].reshape(1, gw),
            "lru_wa": _block_diag(lru_wa[l]).astype(BF16), "lru_ba": lru_ba[l].reshape(1, gw),
            "lru_wx": _block_diag(lru_wx[l]).astype(BF16), "lru_bx": lru_bx[l].reshape(1, gw),
            "lru_lambda": lru_lambda[l].reshape(1, gw), "gmat64": gmat64,
            "norm_a": norm_a[l].reshape(1, gw),
            "gdn_conv_w": gdn_conv_w[l],
            "gdn_aneg": _small_vec(-jnp.exp(gdn_a_log[l]), SMALL_ALPHA),
            "gdn_dt": _small_vec(gdn_dt_bias[l], SMALL_ALPHA),
            "gdn_norm": gdn_norm[l].reshape(1, HEAD_DIM),
        }
        z = in_proj_call(h, _prep_w_in(w_in[l]))

        y_a = lru_call(z, p, batch, seq)

        c = logf_cumsum_call(z, _small_vec(fox_f_bias[l], SMALL_F), batch, seq)
        c = c.reshape(batch, seq, LANE)[:, :, SMALL_F:SMALL_F + N_HEADS]
        c = jnp.transpose(c, (0, 2, 1))
        tq = min(512, seq)
        y_b = fox_call(z.reshape(batch, seq, Z_COLS), c.reshape(batch, N_HEADS, seq, 1),
                       c.reshape(batch, N_HEADS, seq // tq, 1, tq),
                       norm_b[l].reshape(1, gw), batch, seq, tq=tq).reshape(m, gw)

        y_c = gdn_call(z, p, batch, seq)

        branches = [band_call(z, batch, seq, dil) for dil in DILATIONS]
        y_d = merge_call(branches, norm_d[l].reshape(1, gw))

        x1, hf = out_proj_call((y_a, y_b, y_c, y_d), w_out[l].astype(BF16), xs,
                               norm_ffn[l].reshape(1, d))
        g = ffn_up_call(hf, ffn_w_up[l].astype(BF16), ffn_conv_w[l],
                        ffn_conv_b[l].reshape(1, 2 * D_FF), seq)
        last = l == depth - 1
        gain_next = norm_final if last else norm_mix[l + 1]
        xs, h = ffn_down_call(g, ffn_w_down[l].astype(BF16), x1, gain_next.reshape(1, d),
                              F32 if last else BF16)
    return h.reshape(batch, seq, d)
```

```python
import functools
import math

import jax
import jax.numpy as jnp
from jax import lax
from jax.experimental import pallas as pl
from jax.experimental.pallas import tpu as pltpu

F32 = jnp.float32
BF16 = jnp.bfloat16

D_MODEL = 2048
GROUP_WIDTH = 512
HEAD_DIM = 128
N_HEADS = 4
LRU_BLOCKS = 8
LRU_BLOCK_DIM = 64
LRU_C = 8.0
SHORT_CONV = 4
FFN_CONV = 3
D_FF = 5632
GDN_CHUNK = 64
SPAN = 128
DILATIONS = (1, 4, 16)
EPS = 1e-6
NEG_INF = -1e30
ATTN_SCALE = HEAD_DIM ** -0.5

LANE = 128
SUBLANE = 8
VMEM_LIMIT = 52 * 1024 * 1024

COL_CQKV = 0
COL_AX = 1536
COL_AG = 2048
COL_CZ = 2560
COL_BQKV = 3072
COL_DQKV = 4608
COL_SMALL = 6144
Z_COLS = 6400
SMALL_F, SMALL_BETA, SMALL_ALPHA = 0, 4, 8


def _params(*sem):
    return pltpu.CompilerParams(dimension_semantics=sem, vmem_limit_bytes=VMEM_LIMIT)


def _sigmoid(x):
    return 1.0 / (1.0 + jnp.exp(-x))


def _softplus(x):
    return jnp.maximum(x, 0.0) + jnp.log1p(jnp.exp(-jnp.abs(x)))


def _silu(x):
    return x * _sigmoid(x)


def _gelu_tanh(x):
    c = math.sqrt(2.0 / math.pi)
    return 0.5 * x * (1.0 + jnp.tanh(c * (x + 0.044715 * (x * x * x))))


def _dot(a, b, precision=None):
    return jnp.dot(a, b, preferred_element_type=F32, precision=precision)


def _dot_nt(a, b):
    return lax.dot_general(a, b, (((1,), (1,)), ((), ())), preferred_element_type=F32)


def _dot_tn(a, b):
    return lax.dot_general(a, b, (((0,), (0,)), ((), ())), preferred_element_type=F32)


def _causal_conv(x, halo, w, taps):
    n = x.shape[0]
    cat = jnp.concatenate([halo, x], axis=0)
    y = w[taps - 1:taps, :] * x
    for k in range(taps - 1):
        shifted = pltpu.roll(cat, taps - 1 - k, 0)[SUBLANE:SUBLANE + n, :]
        y = y + w[k:k + 1, :] * shifted
    return y


def _scan_affine(a, u):
    n = a.shape[0]
    row = lax.broadcasted_iota(jnp.int32, a.shape, 0)
    s = 1
    while s < n:
        keep = row >= s
        a_s = jnp.where(keep, pltpu.roll(a, s, 0), 1.0)
        u_s = jnp.where(keep, pltpu.roll(u, s, 0), 0.0)
        u = a * u_s + u
        a = a * a_s
        s *= 2
    return a, u


def _scan_add(x, seg):
    row = lax.broadcasted_iota(jnp.int32, x.shape, 0)
    pos = row % seg
    s = 1
    while s < seg:
        x = x + jnp.where(pos >= s, pltpu.roll(x, s, 0), 0.0)
        s *= 2
    return x


def _rmsnorm_rows(x, gain):
    ms = jnp.mean(x * x, axis=-1, keepdims=True)
    return x * lax.rsqrt(ms + EPS) * gain


def _rmsnorm_kernel(x_ref, g_ref, o_ref):
    o_ref[...] = _rmsnorm_rows(x_ref[...], g_ref[...]).astype(o_ref.dtype)


def rmsnorm_call(x, gain, out_dtype, tm=512):
    m, d = x.shape
    return pl.pallas_call(
        _rmsnorm_kernel,
        grid=(m // tm,),
        in_specs=[pl.BlockSpec((tm, d), lambda i: (i, 0)),
                  pl.BlockSpec((1, d), lambda i: (0, 0))],
        out_specs=pl.BlockSpec((tm, d), lambda i: (i, 0)),
        out_shape=jax.ShapeDtypeStruct((m, d), out_dtype),
        compiler_params=_params("parallel"),
        name="rmsnorm",
    )(x, gain.reshape(1, d))


def _matmul_kernel(a_ref, b_ref, o_ref):
    o_ref[...] = _dot(a_ref[...], b_ref[...])


def in_proj_call(h, w, tm=1024, tn=1280):
    m, k = h.shape
    n = w.shape[1]
    tm = min(tm, m)
    return pl.pallas_call(
        _matmul_kernel,
        grid=(m // tm, n // tn),
        in_specs=[pl.BlockSpec((tm, k), lambda i, j: (i, 0)),
                  pl.BlockSpec((k, tn), lambda i, j: (0, j))],
        out_specs=pl.BlockSpec((tm, tn), lambda i, j: (i, j)),
        out_shape=jax.ShapeDtypeStruct((m, n), F32),
        compiler_params=_params("parallel", "arbitrary"),
        name="in_proj",
    )(h, w)


def _group_mean_sq(h, gmat):
    h2 = h * h
    hi = h2.astype(BF16)
    lo = (h2 - hi.astype(F32)).astype(BF16)
    return _dot(hi, gmat) + _dot(lo, gmat)


def _lru_kernel(ax_ref, ag_ref, cw_ref, cb_ref, wa_ref, ba_ref, wx_ref, bx_ref, lam_ref,
                gmat_ref, gain_ref, o_ref, halo_ref, hprev_ref):
    tt = ax_ref.shape[0]

    @pl.when(pl.program_id(1) == 0)
    def _():
        halo_ref[...] = jnp.zeros_like(halo_ref)
        hprev_ref[...] = jnp.zeros_like(hprev_ref)

    x = ax_ref[...]
    xc = _causal_conv(x, halo_ref[...], cw_ref[...], SHORT_CONV) + cb_ref[...]
    halo_ref[...] = x[tt - SUBLANE:, :]
    xb = xc.astype(BF16)
    r = _sigmoid(_dot(xb, wa_ref[...]) + ba_ref[...])
    i = _sigmoid(_dot(xb, wx_ref[...]) + bx_ref[...])
    log_a = (-LRU_C) * r * _softplus(-lam_ref[...])
    a = jnp.exp(log_a)
    u = jnp.sqrt(-jnp.tanh(log_a) * (a * a + 1.0)) * (i * xc)
    a_cum, h = _scan_affine(a, u)
    h = h + a_cum * hprev_ref[...]
    hprev_ref[...] = h[tt - 1:tt, :]
    y = h * lax.rsqrt(_group_mean_sq(h, gmat_ref[...]) + EPS) * gain_ref[...]
    o_ref[...] = (y * _gelu_tanh(ag_ref[...])).astype(o_ref.dtype)


def lru_call(z, p, batch, seq, tt=512):
    tt = min(tt, seq)
    nt = seq // tt
    w = GROUP_WIDTH
    row = lambda b, t: (b * nt + t)
    vec = pl.BlockSpec((1, w), lambda b, t: (0, 0))
    mat = pl.BlockSpec((w, w), lambda b, t: (0, 0))
    return pl.pallas_call(
        _lru_kernel,
        grid=(batch, nt),
        in_specs=[pl.BlockSpec((tt, w), lambda b, t: (row(b, t), COL_AX // w)),
                  pl.BlockSpec((tt, w), lambda b, t: (row(b, t), COL_AG // w)),
                  pl.BlockSpec((SHORT_CONV, w), lambda b, t: (0, 0)),
                  vec, mat, vec, mat, vec, vec, mat, vec],
        out_specs=pl.BlockSpec((tt, w), lambda b, t: (row(b, t), 0)),
        out_shape=jax.ShapeDtypeStruct((batch * seq, w), BF16),
        scratch_shapes=[pltpu.VMEM((SUBLANE, w), F32), pltpu.VMEM((1, w), F32)],
        compiler_params=_params("parallel", "arbitrary"),
        name="rg_lru",
    )(z, z, p["lru_conv_w"], p["lru_conv_b"], p["lru_wa"], p["lru_ba"], p["lru_wx"],
      p["lru_bx"], p["lru_lambda"], p["gmat64"], p["norm_a"])


def _logf_cumsum_kernel(s_ref, bias_ref, o_ref, carry_ref):
    tt = s_ref.shape[0]

    @pl.when(pl.program_id(1) == 0)
    def _():
        carry_ref[...] = jnp.zeros_like(carry_ref)

    log_f = -_softplus(-(s_ref[...] + bias_ref[...]))
    c = _scan_add(log_f, tt) + carry_ref[...]
    carry_ref[...] = c[tt - 1:tt, :]
    o_ref[...] = c


def logf_cumsum_call(z, bias, batch, seq, tt=1024):
    tt = min(tt, seq)
    nt = seq // tt
    return pl.pallas_call(
        _logf_cumsum_kernel,
        grid=(batch, nt),
        in_specs=[pl.BlockSpec((tt, LANE), lambda b, t: (b * nt + t, COL_SMALL // LANE)),
                  pl.BlockSpec((1, LANE), lambda b, t: (0, 0))],
        out_specs=pl.BlockSpec((tt, LANE), lambda b, t: (b * nt + t, 0)),
        out_shape=jax.ShapeDtypeStruct((batch * seq, LANE), F32),
        scratch_shapes=[pltpu.VMEM((1, LANE), F32)],
        compiler_params=_params("parallel", "arbitrary"),
        name="logf_cumsum",
    )(z, bias)


def _fox_kernel(q_ref, k_ref, v_ref, cq_ref, ck_ref, gain_ref, o_ref, kb_ref, vb_ref, *,
                tq, tk):
    qi = pl.program_id(2)

    @pl.when(qi == 0)
    def _():
        kb_ref[...] = k_ref[...].astype(BF16)
        vb_ref[:, :HEAD_DIM] = v_ref[...].astype(BF16)
        vb_ref[:, HEAD_DIM:] = jnp.ones((vb_ref.shape[0], HEAD_DIM), BF16)

    q = (q_ref[...] * ATTN_SCALE).astype(BF16)
    cq = jnp.broadcast_to(cq_ref[...], (tq, LANE))

    def lanes(x, n):
        return jnp.concatenate([x] * (n // LANE), axis=1)

    def scores(kv):
        start = pl.multiple_of(kv * tk, tk)
        return _dot_nt(q, kb_ref[pl.ds(start, tk), :]) - ck_ref[kv]

    def update(kv, s, m, acc):
        start = pl.multiple_of(kv * tk, tk)
        m_new = jnp.maximum(m, jnp.max(s, axis=-1, keepdims=True) + cq)
        p = jnp.exp(s - lanes(m_new - cq, tk))
        alpha = jnp.exp(m - m_new)
        pv = _dot(p.astype(BF16), vb_ref[pl.ds(start, tk), :])
        return m_new, lanes(alpha, 2 * HEAD_DIM) * acc + pv

    def body(j, carry):
        s, m, acc = carry
        s_next = scores(j + 1)
        m, acc = update(j, s, m, acc)
        return s_next, m, acc

    init = (scores(0), jnp.full((tq, LANE), NEG_INF, F32), jnp.zeros((tq, 2 * HEAD_DIM), F32))
    s, m, acc = lax.fori_loop(0, qi, body, init)
    row = lax.broadcasted_iota(jnp.int32, s.shape, 0)
    col = lax.broadcasted_iota(jnp.int32, s.shape, 1)
    _, acc = update(qi, jnp.where(col <= row, s, NEG_INF), m, acc)
    o = acc[:, :HEAD_DIM] / acc[:, HEAD_DIM:]
    o = o * lax.rsqrt(jnp.mean(o * o, axis=-1, keepdims=True) + EPS) * gain_ref[...]
    o_ref[...] = o.astype(o_ref.dtype)


def fox_call(z3, c, gain, batch, seq, tq=512):
    tq = min(tq, seq)
    tk = tq
    nq = seq // tq
    nk = seq // tk
    qb, kb, vb = (COL_BQKV // LANE, (COL_BQKV + GROUP_WIDTH) // LANE,
                  (COL_BQKV + 2 * GROUP_WIDTH) // LANE)
    c_col = c.reshape(batch, N_HEADS, seq, 1)
    c_row = c.reshape(batch, N_HEADS, nk, 1, tk)
    return pl.pallas_call(
        functools.partial(_fox_kernel, tq=tq, tk=tk),
        grid=(batch, N_HEADS, nq),
        in_specs=[pl.BlockSpec((None, tq, LANE), lambda b, h, i: (b, i, qb + h)),
                  pl.BlockSpec((None, seq, LANE), lambda b, h, i: (b, 0, kb + h)),
                  pl.BlockSpec((None, seq, LANE), lambda b, h, i: (b, 0, vb + h)),
                  pl.BlockSpec((None, None, tq, 1), lambda b, h, i: (b, h, i, 0)),
                  pl.BlockSpec((None, None, nk, 1, tk), lambda b, h, i: (b, h, 0, 0, 0)),
                  pl.BlockSpec((1, LANE), lambda b, h, i: (0, h))],
        out_specs=pl.BlockSpec((None, tq, LANE), lambda b, h, i: (b, i, h)),
        out_shape=jax.ShapeDtypeStruct((batch, seq, GROUP_WIDTH), BF16),
        scratch_shapes=[pltpu.VMEM((seq, LANE), BF16), pltpu.VMEM((seq, 2 * LANE), BF16)],
        compiler_params=_params("parallel", "parallel", "arbitrary"),
        name="fox_attention",
    )(z3, z3, z3, c_col, c_row, gain)


def _l2norm(t):
    return t * lax.rsqrt(jnp.sum(t * t, axis=-1, keepdims=True) + EPS)


def _bmm(a, b):
    return lax.dot_general(a, b, (((2,), (1,)), ((0,), (0,))), preferred_element_type=F32)


def _bmm_nt(a, b):
    return lax.dot_general(a, b, (((2,), (2,)), ((0,), (0,))), preferred_element_type=F32)


def _bmm_tn(a, b):
    return lax.dot_general(a, b, (((1,), (1,)), ((0,), (0,))), preferred_element_type=F32)


def _neumann_tail(p):
    c = p.shape[-1]
    a = p
    q = p
    cover = 2
    while cover < c:
        qb = q.astype(BF16)
        q = _bmm(qb, qb)
        a = a + q + _bmm(a.astype(BF16), q.astype(BF16))
        cover *= 2
    return a


def _gdn_kernel(x_ref, zg_ref, s_ref, cw_ref, aneg_ref, dt_ref, gain_ref, o_ref,
                halo_ref, qkv_ref, state_ref):
    tt = x_ref.shape[0]
    c = GDN_CHUNK
    nc = tt // c
    w = GROUP_WIDTH
    hd = HEAD_DIM

    @pl.when(pl.program_id(1) == 0)
    def _():
        halo_ref[...] = jnp.zeros_like(halo_ref)
        state_ref[...] = jnp.zeros_like(state_ref)

    x = x_ref[...]
    qkv_ref[...] = _silu(_causal_conv(x, halo_ref[...], cw_ref[...], SHORT_CONV))
    halo_ref[...] = x[tt - SUBLANE:, :]
    small = s_ref[...]
    beta_all = _sigmoid(small)
    gc_all = _scan_add(aneg_ref[...] * _softplus(small + dt_ref[...]), c)

    pairs = [(ci, h) for ci in range(nc) for h in range(N_HEADS)]

    def gather(col0):
        return jnp.stack([qkv_ref[ci * c:(ci + 1) * c, col0 + h * hd:col0 + (h + 1) * hd]
                          for ci, h in pairs])

    q = _l2norm(gather(0)) * ATTN_SCALE
    k = _l2norm(gather(w))
    v = gather(2 * w)
    beta = jnp.stack([beta_all[ci * c:(ci + 1) * c, SMALL_BETA + h:SMALL_BETA + h + 1]
                      for ci, h in pairs])
    gc = jnp.stack([gc_all[ci * c:(ci + 1) * c, SMALL_ALPHA + h:SMALL_ALPHA + h + 1]
                    for ci, h in pairs])
    gc_t = [gc_all[ci * c:(ci + 1) * c, :].T for ci in range(nc)]
    gc_row = jnp.stack([gc_t[ci][SMALL_ALPHA + h:SMALL_ALPHA + h + 1, :]
                        for ci, h in pairs])

    row = lax.broadcasted_iota(jnp.int32, (c, c), 0)
    col = lax.broadcasted_iota(jnp.int32, (c, c), 1)
    tril = col <= row
    strict = col < row
    decay = jnp.where(tril, jnp.exp(jnp.where(tril, gc - gc_row, 0.0)), 0.0)
    kb = k.astype(BF16)
    kbeta = k * beta
    kk = _bmm_nt(kbeta.astype(BF16), kb) * decay
    tail = _neumann_tail(jnp.where(strict, -kk, 0.0))
    rhs = jnp.concatenate([v * beta, kbeta * jnp.exp(gc)], axis=-1)
    sol = rhs + _bmm(tail.astype(BF16), rhs.astype(BF16))
    u = sol[:, :, :hd]
    wy = sol[:, :, hd:]
    qk = jnp.where(tril, _bmm_nt(q.astype(BF16), kb) * decay, 0.0).astype(BF16)
    g_last = gc[:, c - 1:c, :]
    k_dec = (k * jnp.exp(g_last - gc)).astype(BF16)
    e_last = jnp.exp(g_last)
    wq = jnp.concatenate([wy.astype(BF16), (q * jnp.exp(gc)).astype(BF16)], axis=1)

    state = state_ref[...]
    for ci in range(nc):
        sl = slice(ci * N_HEADS, (ci + 1) * N_HEADS)
        ws = _bmm(wq[sl], state.astype(BF16))
        v_new = (u[sl] - ws[:, :c, :]).astype(BF16)
        o = ws[:, c:, :] + _bmm(qk[sl], v_new)
        state = state * e_last[sl] + _bmm_tn(k_dec[sl], v_new)
        o = o * lax.rsqrt(jnp.mean(o * o, axis=-1, keepdims=True) + EPS) * gain_ref[...]
        for h in range(N_HEADS):
            zg = zg_ref[ci * c:(ci + 1) * c, h * hd:(h + 1) * hd]
            o_ref[ci * c:(ci + 1) * c, h * hd:(h + 1) * hd] = (o[h] * _silu(zg)).astype(o_ref.dtype)
    state_ref[...] = state


def gdn_call(z, p, batch, seq, tt=512):
    tt = min(tt, seq)
    nt = seq // tt
    w = GROUP_WIDTH
    vec = pl.BlockSpec((1, LANE), lambda b, t: (0, 0))
    return pl.pallas_call(
        _gdn_kernel,
        grid=(batch, nt),
        in_specs=[pl.BlockSpec((tt, 3 * w), lambda b, t: (b * nt + t, COL_CQKV // (3 * w))),
                  pl.BlockSpec((tt, w), lambda b, t: (b * nt + t, COL_CZ // w)),
                  pl.BlockSpec((tt, LANE), lambda b, t: (b * nt + t, COL_SMALL // LANE)),
                  pl.BlockSpec((SHORT_CONV, 3 * w), lambda b, t: (0, 0)),
                  vec, vec, vec],
        out_specs=pl.BlockSpec((tt, w), lambda b, t: (b * nt + t, 0)),
        out_shape=jax.ShapeDtypeStruct((batch * seq, w), BF16),
        scratch_shapes=[pltpu.VMEM((SUBLANE, 3 * w), F32),
                        pltpu.VMEM((tt, 3 * w), F32),
                        pltpu.VMEM((N_HEADS, HEAD_DIM, HEAD_DIM), F32)],
        compiler_params=_params("parallel", "arbitrary"),
        name="gated_deltanet",
    )(z, z, z, p["gdn_conv_w"], p["gdn_aneg"], p["gdn_dt"], p["gdn_norm"])


def _band_kernel(q_ref, kc_ref, kp_ref, vc_ref, vp_ref, o_ref, lse_ref):
    tq = q_ref.shape[0]
    first = pl.program_id(3) == 0
    kcat = jnp.concatenate([kp_ref[...], kc_ref[...]], axis=0).astype(BF16)
    vcat = jnp.concatenate([vp_ref[...], vc_ref[...]], axis=0).astype(BF16)
    i = lax.broadcasted_iota(jnp.int32, (SPAN, 2 * SPAN), 0)
    j = lax.broadcasted_iota(jnp.int32, (SPAN, 2 * SPAN), 1)
    dist = i + SPAN - j
    band = (dist >= 0) & (dist <= SPAN)
    for sb in range(tq // SPAN):
        q = (q_ref[sb * SPAN:(sb + 1) * SPAN, :] * ATTN_SCALE).astype(BF16)
        k = kcat[sb * SPAN:(sb + 2) * SPAN, :]
        v = vcat[sb * SPAN:(sb + 2) * SPAN, :]
        s = _dot_nt(q, k)
        mask = band
        if sb == 0:
            mask = band & jnp.logical_or(j >= SPAN, jnp.logical_not(first))
        s = jnp.where(mask, s, NEG_INF)
        m = jnp.max(s, axis=-1, keepdims=True)
        p = jnp.exp(s - m)
        den = jnp.sum(p, axis=-1, keepdims=True)
        o_ref[sb * SPAN:(sb + 1) * SPAN, :] = _dot(p.astype(BF16), v) / den
        lse_ref[sb * SPAN:(sb + 1) * SPAN, :] = jnp.broadcast_to(m + jnp.log(den), (SPAN, LANE))


def band_call(z, batch, seq, dil, tq=512):
    td = seq // dil
    tq = min(tq, td)
    nq = td // tq
    zc = Z_COLS // LANE
    oc = GROUP_WIDTH // LANE
    zv = z.reshape(batch, td, dil * Z_COLS)
    qb, kb, vb = (COL_DQKV // LANE, (COL_DQKV + GROUP_WIDTH) // LANE,
                  (COL_DQKV + 2 * GROUP_WIDTH) // LANE)
    ratio = tq // SPAN

    def cur(off):
        return pl.BlockSpec((None, tq, LANE), lambda b, h, r, i: (b, i, r * zc + off + h))

    def prev(off):
        return pl.BlockSpec((None, SPAN, LANE),
                            lambda b, h, r, i: (b, jnp.maximum(i * ratio - 1, 0), r * zc + off + h))

    out_spec = pl.BlockSpec((None, tq, LANE), lambda b, h, r, i: (b, i, r * oc + h))
    shape = jax.ShapeDtypeStruct((batch, td, dil * GROUP_WIDTH), F32)
    o, lse = pl.pallas_call(
        _band_kernel,
        grid=(batch, N_HEADS, dil, nq),
        in_specs=[cur(qb), cur(kb), prev(kb), cur(vb), prev(vb)],
        out_specs=[out_spec, out_spec],
        out_shape=[shape, shape],
        compiler_params=_params("parallel", "parallel", "parallel", "arbitrary"),
        name="dilated_band_%d" % dil,
    )(zv, zv, zv, zv, zv)
    return (o.reshape(batch * seq, GROUP_WIDTH), lse.reshape(batch * seq, GROUP_WIDTH))


def _merge_kernel(o0, l0, o1, l1, o2, l2, gain_ref, y_ref):
    lses = (l0[...], l1[...], l2[...])
    m = jnp.maximum(jnp.maximum(lses[0], lses[1]), lses[2])
    es = [jnp.exp(l - m) for l in lses]
    den = es[0] + es[1] + es[2]
    y = (es[0] / den) * o0[...] + (es[1] / den) * o1[...] + (es[2] / den) * o2[...]
    for h in range(N_HEADS):
        yh = y[:, h * HEAD_DIM:(h + 1) * HEAD_DIM]
        yh = yh * lax.rsqrt(jnp.mean(yh * yh, axis=-1, keepdims=True) + EPS)
        y_ref[:, h * HEAD_DIM:(h + 1) * HEAD_DIM] = (
            yh * gain_ref[:, h * HEAD_DIM:(h + 1) * HEAD_DIM]).astype(y_ref.dtype)


def merge_call(branches, gain, tm=512):
    m, w = branches[0][0].shape
    tm = min(tm, m)
    blk = pl.BlockSpec((tm, w), lambda i: (i, 0))
    flat = [a for pair in branches for a in pair]
    return pl.pallas_call(
        _merge_kernel,
        grid=(m // tm,),
        in_specs=[blk] * 6 + [pl.BlockSpec((1, w), lambda i: (0, 0))],
        out_specs=blk,
        out_shape=jax.ShapeDtypeStruct((m, w), BF16),
        compiler_params=_params("parallel"),
        name="dilated_merge",
    )(*flat, gain)


def _out_proj_kernel(ya, yb, yc, yd, w_ref, x_ref, g_ref, x1_ref, h_ref):
    w = GROUP_WIDTH
    acc = x_ref[...]
    for n, y in enumerate((ya, yb, yc, yd)):
        acc = acc + _dot(y[...], w_ref[n * w:(n + 1) * w, :])
    x1_ref[...] = acc
    h_ref[...] = _rmsnorm_rows(acc, g_ref[...]).astype(h_ref.dtype)


def out_proj_call(ys, w_out, x, gain, tm=512):
    m, d = x.shape
    tm = min(tm, m)
    yblk = pl.BlockSpec((tm, GROUP_WIDTH), lambda i: (i, 0))
    xblk = pl.BlockSpec((tm, d), lambda i: (i, 0))
    return pl.pallas_call(
        _out_proj_kernel,
        grid=(m // tm,),
        in_specs=[yblk] * 4 + [pl.BlockSpec((d, d), lambda i: (0, 0),
                                            pipeline_mode=pl.Buffered(1)), xblk,
                               pl.BlockSpec((1, d), lambda i: (0, 0))],
        out_specs=[xblk, xblk],
        out_shape=[jax.ShapeDtypeStruct((m, d), F32), jax.ShapeDtypeStruct((m, d), BF16)],
        compiler_params=_params("parallel"),
        name="out_proj",
    )(*ys, w_out, x, gain)


def _ffn_up_kernel(h_ref, wu_ref, wg_ref, cwu_ref, cwg_ref, cbu_ref, cbg_ref, o_ref,
                   tail_u_ref, tail_g_ref, *, tiles_per_seq):
    tm = h_ref.shape[0]

    @pl.when(pl.program_id(1) % tiles_per_seq == 0)
    def _():
        tail_u_ref[...] = jnp.zeros_like(tail_u_ref)
        tail_g_ref[...] = jnp.zeros_like(tail_g_ref)

    h = h_ref[...]
    u = _dot(h, wu_ref[...])
    g = _dot(h, wg_ref[...])
    up = _causal_conv(u, tail_u_ref[...], cwu_ref[...], FFN_CONV) + cbu_ref[...]
    gate = _causal_conv(g, tail_g_ref[...], cwg_ref[...], FFN_CONV) + cbg_ref[...]
    tail_u_ref[...] = u[tm - SUBLANE:, :]
    tail_g_ref[...] = g[tm - SUBLANE:, :]
    o_ref[...] = (_silu(gate) * up).astype(o_ref.dtype)


def ffn_up_call(h, w_up, conv_w, conv_b, seq, tm=512, tn=512):
    m, d = h.shape
    tm = min(tm, seq)
    nj = D_FF // tn
    return pl.pallas_call(
        functools.partial(_ffn_up_kernel, tiles_per_seq=seq // tm),
        grid=(nj, m // tm),
        in_specs=[pl.BlockSpec((tm, d), lambda j, i: (i, 0)),
                  pl.BlockSpec((d, tn), lambda j, i: (0, j)),
                  pl.BlockSpec((d, tn), lambda j, i: (0, j + nj)),
                  pl.BlockSpec((FFN_CONV, tn), lambda j, i: (0, j)),
                  pl.BlockSpec((FFN_CONV, tn), lambda j, i: (0, j + nj)),
                  pl.BlockSpec((1, tn), lambda j, i: (0, j)),
                  pl.BlockSpec((1, tn), lambda j, i: (0, j + nj))],
        out_specs=pl.BlockSpec((tm, tn), lambda j, i: (i, j)),
        out_shape=jax.ShapeDtypeStruct((m, D_FF), BF16),
        scratch_shapes=[pltpu.VMEM((SUBLANE, tn), F32), pltpu.VMEM((SUBLANE, tn), F32)],
        compiler_params=_params("parallel", "arbitrary"),
        name="ffn_up",
    )(h, w_up, w_up, conv_w, conv_w, conv_b, conv_b)


def _ffn_down_kernel(g_ref, w_ref, x_ref, gain_ref, x2_ref, hn_ref):
    x2 = x_ref[...] + _dot(g_ref[...], w_ref[...])
    x2_ref[...] = x2
    hn_ref[...] = _rmsnorm_rows(x2, gain_ref[...]).astype(hn_ref.dtype)


def ffn_down_call(g, w_down, x, gain, hn_dtype, tm=256):
    m, d = x.shape
    tm = min(tm, m)
    kdim = g.shape[1]
    xblk = pl.BlockSpec((tm, d), lambda i: (i, 0))
    return pl.pallas_call(
        _ffn_down_kernel,
        grid=(m // tm,),
        in_specs=[pl.BlockSpec((tm, kdim), lambda i: (i, 0)),
                  pl.BlockSpec((kdim, d), lambda i: (0, 0), pipeline_mode=pl.Buffered(1)),
                  xblk, pl.BlockSpec((1, d), lambda i: (0, 0))],
        out_specs=[xblk, xblk],
        out_shape=[jax.ShapeDtypeStruct((m, d), F32), jax.ShapeDtypeStruct((m, d), hn_dtype)],
        compiler_params=_params("parallel"),
        name="ffn_down",
    )(g, w_down, x, gain)


def _block_diag(blocks):
    n, r, c = blocks.shape
    eye = jnp.eye(n, dtype=blocks.dtype)
    return (eye[:, None, :, None] * blocks[:, :, None, :]).reshape(n * r, n * c)


def _small_vec(values, offset):
    return jnp.zeros((1, LANE), F32).at[0, offset:offset + values.shape[0]].set(values)


def _w_in_segments():
    gw = GROUP_WIDTH
    sizes = (gw, gw, 3 * gw, N_HEADS, 3 * gw, gw, N_HEADS, N_HEADS, 3 * gw)
    dsts = (COL_AX, COL_AG, COL_BQKV, COL_SMALL + SMALL_F, COL_CQKV, COL_CZ,
            COL_SMALL + SMALL_BETA, COL_SMALL + SMALL_ALPHA, COL_DQKV)
    segs, src = [], 0
    for size, dst in zip(sizes, dsts):
        segs.append((src, dst, size))
        src += size
    return segs, src


def _prep_w_in_kernel(w_ref, o_ref):
    segs, _ = _w_in_segments()
    o_ref[:, COL_SMALL:] = jnp.zeros((o_ref.shape[0], Z_COLS - COL_SMALL), o_ref.dtype)
    for src, dst, size in segs:
        lo = (src // LANE) * LANE
        hi = min(-(-(src + size) // LANE) * LANE, w_ref.shape[1])
        tile = w_ref[:, lo:hi]
        o_ref[:, dst:dst + size] = tile[:, src - lo:src - lo + size].astype(o_ref.dtype)


def prep_w_in_call(w_in, tk=256):
    depth, k, n = w_in.shape
    assert n == _w_in_segments()[1]
    return pl.pallas_call(
        _prep_w_in_kernel,
        grid=(depth, k // tk),
        in_specs=[pl.BlockSpec((None, tk, n), lambda l, i: (l, i, 0))],
        out_specs=pl.BlockSpec((None, tk, Z_COLS), lambda l, i: (l, i, 0)),
        out_shape=jax.ShapeDtypeStruct((depth, k, Z_COLS), BF16),
        compiler_params=_params("parallel", "parallel"),
        name="prep_w_in",
    )(w_in)


def kernel(x, norm_mix, w_in, lru_conv_w, lru_conv_b, lru_wa, lru_ba, lru_wx, lru_bx,
           lru_lambda, fox_f_bias, gdn_conv_w, gdn_a_log, gdn_dt_bias, gdn_norm,
           norm_a, norm_b, norm_d, w_out, norm_ffn, ffn_w_up, ffn_conv_w, ffn_conv_b,
           ffn_w_down, norm_final):
    batch, seq, d = x.shape
    depth = w_in.shape[0]
    m = batch * seq
    gw = GROUP_WIDTH
    gmat64 = _block_diag(jnp.full((LRU_BLOCKS, LRU_BLOCK_DIM, LRU_BLOCK_DIM),
                                  1.0 / LRU_BLOCK_DIM, F32)).astype(BF16)
    xs = x.reshape(m, d)
    w_in_z = prep_w_in_call(w_in)
    h = rmsnorm_call(xs, norm_mix[0], BF16)
    for l in range(depth):
        p = {
            "lru_conv_w": lru_conv_w[l], "lru_conv_b": lru_conv_b[l].reshape(1, gw),
            "lru_wa": _block_diag(lru_wa[l]).astype(BF16), "lru_ba": lru_ba[l].reshape(1, gw),
            "lru_wx": _block_diag(lru_wx[l]).astype(BF16), "lru_bx": lru_bx[l].reshape(1, gw),
            "lru_lambda": lru_lambda[l].reshape(1, gw), "gmat64": gmat64,
            "norm_a": norm_a[l].reshape(1, gw),
            "gdn_conv_w": gdn_conv_w[l],
            "gdn_aneg": _small_vec(-jnp.exp(gdn_a_log[l]), SMALL_ALPHA),
            "gdn_dt": _small_vec(gdn_dt_bias[l], SMALL_ALPHA),
            "gdn_norm": gdn_norm[l].reshape(1, HEAD_DIM),
        }
        z = in_proj_call(h, w_in_z[l])

        y_a = lru_call(z, p, batch, seq)

        c = logf_cumsum_call(z, _small_vec(fox_f_bias[l], SMALL_F), batch, seq)
        c = c.reshape(batch, seq, LANE)[:, :, SMALL_F:SMALL_F + N_HEADS]
        c = jnp.transpose(c, (0, 2, 1))
        y_b = fox_call(z.reshape(batch, seq, Z_COLS), c, norm_b[l].reshape(1, gw),
                       batch, seq).reshape(m, gw)

        y_c = gdn_call(z, p, batch, seq)

        branches = [band_call(z, batch, seq, dil) for dil in DILATIONS]
        y_d = merge_call(branches, norm_d[l].reshape(1, gw))

        x1, hf = out_proj_call((y_a, y_b, y_c, y_d), w_out[l].astype(BF16), xs,
                               norm_ffn[l].reshape(1, d))
        g = ffn_up_call(hf, ffn_w_up[l].astype(BF16), ffn_conv_w[l],
                        ffn_conv_b[l].reshape(1, 2 * D_FF), seq)
        last = l == depth - 1
        gain_next = norm_final if last else norm_mix[l + 1]
        xs, h = ffn_down_call(g, ffn_w_down[l].astype(BF16), x1, gain_next.reshape(1, d),
                              F32 if last else BF16)
    return h.reshape(batch, seq, d)
```

```python
import functools
import math

import jax
import jax.numpy as jnp
from jax import lax
from jax.experimental import pallas as pl
from jax.experimental.pallas import tpu as pltpu

F32 = jnp.float32
BF16 = jnp.bfloat16

D_MODEL = 2048
GROUP_WIDTH = 512
HEAD_DIM = 128
N_HEADS = 4
LRU_BLOCKS = 8
LRU_BLOCK_DIM = 64
LRU_C = 8.0
SHORT_CONV = 4
FFN_CONV = 3
D_FF = 5632
GDN_CHUNK = 64
SPAN = 128
DILATIONS = (1, 4, 16)
EPS = 1e-6
NEG_INF = -1e30
ATTN_SCALE = HEAD_DIM ** -0.5

LANE = 128
SUBLANE = 8
VMEM_LIMIT = 52 * 1024 * 1024

COL_CQKV = 0
COL_AX = 1536
COL_AG = 2048
COL_CZ = 2560
COL_BQKV = 3072
COL_DQKV = 4608
COL_SMALL = 6144
Z_COLS = 6400
SMALL_F, SMALL_BETA, SMALL_ALPHA = 0, 4, 8


def _params(*sem):
    return pltpu.CompilerParams(dimension_semantics=sem, vmem_limit_bytes=VMEM_LIMIT)


def _sigmoid(x):
    return 1.0 / (1.0 + jnp.exp(-x))


def _softplus(x):
    return jnp.maximum(x, 0.0) + jnp.log1p(jnp.exp(-jnp.abs(x)))


def _silu(x):
    return x * _sigmoid(x)


def _gelu_tanh(x):
    c = math.sqrt(2.0 / math.pi)
    return 0.5 * x * (1.0 + jnp.tanh(c * (x + 0.044715 * (x * x * x))))


def _dot(a, b, precision=None):
    return jnp.dot(a, b, preferred_element_type=F32, precision=precision)


def _dot_nt(a, b):
    return lax.dot_general(a, b, (((1,), (1,)), ((), ())), preferred_element_type=F32)


def _dot_tn(a, b):
    return lax.dot_general(a, b, (((0,), (0,)), ((), ())), preferred_element_type=F32)


def _causal_conv(x, halo, w, taps):
    n = x.shape[0]
    cat = jnp.concatenate([halo, x], axis=0)
    y = w[taps - 1:taps, :] * x
    for k in range(taps - 1):
        shifted = pltpu.roll(cat, taps - 1 - k, 0)[SUBLANE:SUBLANE + n, :]
        y = y + w[k:k + 1, :] * shifted
    return y


def _scan_affine(a, u):
    n = a.shape[0]
    row = lax.broadcasted_iota(jnp.int32, a.shape, 0)
    s = 1
    while s < n:
        keep = row >= s
        a_s = jnp.where(keep, pltpu.roll(a, s, 0), 1.0)
        u_s = jnp.where(keep, pltpu.roll(u, s, 0), 0.0)
        u = a * u_s + u
        a = a * a_s
        s *= 2
    return a, u


def _scan_add(x, seg):
    row = lax.broadcasted_iota(jnp.int32, x.shape, 0)
    pos = row % seg
    s = 1
    while s < seg:
        x = x + jnp.where(pos >= s, pltpu.roll(x, s, 0), 0.0)
        s *= 2
    return x


def _rmsnorm_rows(x, gain):
    ms = jnp.mean(x * x, axis=-1, keepdims=True)
    return x * lax.rsqrt(ms + EPS) * gain


def _rmsnorm_kernel(x_ref, g_ref, o_ref):
    o_ref[...] = _rmsnorm_rows(x_ref[...], g_ref[...]).astype(o_ref.dtype)


def rmsnorm_call(x, gain, out_dtype, tm=512):
    m, d = x.shape
    return pl.pallas_call(
        _rmsnorm_kernel,
        grid=(m // tm,),
        in_specs=[pl.BlockSpec((tm, d), lambda i: (i, 0)),
                  pl.BlockSpec((1, d), lambda i: (0, 0))],
        out_specs=pl.BlockSpec((tm, d), lambda i: (i, 0)),
        out_shape=jax.ShapeDtypeStruct((m, d), out_dtype),
        compiler_params=_params("parallel"),
        name="rmsnorm",
    )(x, gain.reshape(1, d))


def _matmul_kernel(a_ref, b_ref, o_ref):
    o_ref[...] = _dot(a_ref[...], b_ref[...])


def in_proj_call(h, w, tm=1024, tn=1280):
    m, k = h.shape
    n = w.shape[1]
    tm = min(tm, m)
    return pl.pallas_call(
        _matmul_kernel,
        grid=(m // tm, n // tn),
        in_specs=[pl.BlockSpec((tm, k), lambda i, j: (i, 0)),
                  pl.BlockSpec((k, tn), lambda i, j: (0, j))],
        out_specs=pl.BlockSpec((tm, tn), lambda i, j: (i, j)),
        out_shape=jax.ShapeDtypeStruct((m, n), F32),
        compiler_params=_params("parallel", "arbitrary"),
        name="in_proj",
    )(h, w)


def _group_mean_sq(h, gmat):
    h2 = h * h
    hi = h2.astype(BF16)
    lo = (h2 - hi.astype(F32)).astype(BF16)
    return _dot(hi, gmat) + _dot(lo, gmat)


def _lru_kernel(ax_ref, ag_ref, cw_ref, cb_ref, wa_ref, ba_ref, wx_ref, bx_ref, lam_ref,
                gmat_ref, gain_ref, o_ref, halo_ref, hprev_ref):
    tt = ax_ref.shape[0]

    @pl.when(pl.program_id(1) == 0)
    def _():
        halo_ref[...] = jnp.zeros_like(halo_ref)
        hprev_ref[...] = jnp.zeros_like(hprev_ref)

    x = ax_ref[...]
    xc = _causal_conv(x, halo_ref[...], cw_ref[...], SHORT_CONV) + cb_ref[...]
    halo_ref[...] = x[tt - SUBLANE:, :]
    xb = xc.astype(BF16)
    r = _sigmoid(_dot(xb, wa_ref[...]) + ba_ref[...])
    i = _sigmoid(_dot(xb, wx_ref[...]) + bx_ref[...])
    log_a = (-LRU_C) * r * _softplus(-lam_ref[...])
    a = jnp.exp(log_a)
    u = jnp.sqrt(-jnp.tanh(log_a) * (a * a + 1.0)) * (i * xc)
    a_cum, h = _scan_affine(a, u)
    h = h + a_cum * hprev_ref[...]
    hprev_ref[...] = h[tt - 1:tt, :]
    y = h * lax.rsqrt(_group_mean_sq(h, gmat_ref[...]) + EPS) * gain_ref[...]
    o_ref[...] = (y * _gelu_tanh(ag_ref[...])).astype(o_ref.dtype)


def lru_call(z, p, batch, seq, tt=512):
    tt = min(tt, seq)
    nt = seq // tt
    w = GROUP_WIDTH
    row = lambda b, t: (b * nt + t)
    vec = pl.BlockSpec((1, w), lambda b, t: (0, 0))
    mat = pl.BlockSpec((w, w), lambda b, t: (0, 0))
    return pl.pallas_call(
        _lru_kernel,
        grid=(batch, nt),
        in_specs=[pl.BlockSpec((tt, w), lambda b, t: (row(b, t), COL_AX // w)),
                  pl.BlockSpec((tt, w), lambda b, t: (row(b, t), COL_AG // w)),
                  pl.BlockSpec((SHORT_CONV, w), lambda b, t: (0, 0)),
                  vec, mat, vec, mat, vec, vec, mat, vec],
        out_specs=pl.BlockSpec((tt, w), lambda b, t: (row(b, t), 0)),
        out_shape=jax.ShapeDtypeStruct((batch * seq, w), BF16),
        scratch_shapes=[pltpu.VMEM((SUBLANE, w), F32), pltpu.VMEM((1, w), F32)],
        compiler_params=_params("parallel", "arbitrary"),
        name="rg_lru",
    )(z, z, p["lru_conv_w"], p["lru_conv_b"], p["lru_wa"], p["lru_ba"], p["lru_wx"],
      p["lru_bx"], p["lru_lambda"], p["gmat64"], p["norm_a"])


def _logf_cumsum_kernel(s_ref, bias_ref, o_ref, carry_ref):
    tt = s_ref.shape[0]

    @pl.when(pl.program_id(1) == 0)
    def _():
        carry_ref[...] = jnp.zeros_like(carry_ref)

    log_f = -_softplus(-(s_ref[...] + bias_ref[...]))
    c = _scan_add(log_f, tt) + carry_ref[...]
    carry_ref[...] = c[tt - 1:tt, :]
    o_ref[...] = c


def logf_cumsum_call(z, bias, batch, seq, tt=1024):
    tt = min(tt, seq)
    nt = seq // tt
    return pl.pallas_call(
        _logf_cumsum_kernel,
        grid=(batch, nt),
        in_specs=[pl.BlockSpec((tt, LANE), lambda b, t: (b * nt + t, COL_SMALL // LANE)),
                  pl.BlockSpec((1, LANE), lambda b, t: (0, 0))],
        out_specs=pl.BlockSpec((tt, LANE), lambda b, t: (b * nt + t, 0)),
        out_shape=jax.ShapeDtypeStruct((batch * seq, LANE), F32),
        scratch_shapes=[pltpu.VMEM((1, LANE), F32)],
        compiler_params=_params("parallel", "arbitrary"),
        name="logf_cumsum",
    )(z, bias)


def _fox_kernel(q_ref, k_ref, v_ref, cq_ref, ck_ref, gain_ref, o_ref, kb_ref, vb_ref, *,
                tq, tk):
    qi = pl.program_id(2)

    @pl.when(qi == 0)
    def _():
        kb_ref[...] = k_ref[...].astype(BF16)
        vb_ref[:, :HEAD_DIM] = v_ref[...].astype(BF16)
        vb_ref[:, HEAD_DIM:] = jnp.ones((vb_ref.shape[0], HEAD_DIM), BF16)

    q = (q_ref[...] * ATTN_SCALE).astype(BF16)
    cq = jnp.broadcast_to(cq_ref[...], (tq, LANE))

    def lanes(x, n):
        return jnp.concatenate([x] * (n // LANE), axis=1)

    def scores(kv):
        start = pl.multiple_of(kv * tk, tk)
        return _dot_nt(q, kb_ref[pl.ds(start, tk), :]) - ck_ref[kv]

    def update(kv, s, m, acc):
        start = pl.multiple_of(kv * tk, tk)
        m_new = jnp.maximum(m, jnp.max(s, axis=-1, keepdims=True) + cq)
        p = jnp.exp(s - lanes(m_new - cq, tk))
        alpha = jnp.exp(m - m_new)
        pv = _dot(p.astype(BF16), vb_ref[pl.ds(start, tk), :])
        return m_new, lanes(alpha, 2 * HEAD_DIM) * acc + pv

    def body(j, carry):
        s, m, acc = carry
        s_next = scores(j + 1)
        m, acc = update(j, s, m, acc)
        return s_next, m, acc

    init = (scores(0), jnp.full((tq, LANE), NEG_INF, F32), jnp.zeros((tq, 2 * HEAD_DIM), F32))
    s, m, acc = lax.fori_loop(0, qi, body, init)
    row = lax.broadcasted_iota(jnp.int32, s.shape, 0)
    col = lax.broadcasted_iota(jnp.int32, s.shape, 1)
    _, acc = update(qi, jnp.where(col <= row, s, NEG_INF), m, acc)
    o = acc[:, :HEAD_DIM] / acc[:, HEAD_DIM:]
    o = o * lax.rsqrt(jnp.mean(o * o, axis=-1, keepdims=True) + EPS) * gain_ref[...]
    o_ref[...] = o.astype(o_ref.dtype)


def fox_call(z3, c, gain, batch, seq, tq=512):
    tq = min(tq, seq)
    tk = tq
    nq = seq // tq
    nk = seq // tk
    qb, kb, vb = (COL_BQKV // LANE, (COL_BQKV + GROUP_WIDTH) // LANE,
                  (COL_BQKV + 2 * GROUP_WIDTH) // LANE)
    c_col = c.reshape(batch, N_HEADS, seq, 1)
    c_row = c.reshape(batch, N_HEADS, nk, 1, tk)
    return pl.pallas_call(
        functools.partial(_fox_kernel, tq=tq, tk=tk),
        grid=(batch, N_HEADS, nq),
        in_specs=[pl.BlockSpec((None, tq, LANE), lambda b, h, i: (b, i, qb + h)),
                  pl.BlockSpec((None, seq, LANE), lambda b, h, i: (b, 0, kb + h)),
                  pl.BlockSpec((None, seq, LANE), lambda b, h, i: (b, 0, vb + h)),
                  pl.BlockSpec((None, None, tq, 1), lambda b, h, i: (b, h, i, 0)),
                  pl.BlockSpec((None, None, nk, 1, tk), lambda b, h, i: (b, h, 0, 0, 0)),
                  pl.BlockSpec((1, LANE), lambda b, h, i: (0, h))],
        out_specs=pl.BlockSpec((None, tq, LANE), lambda b, h, i: (b, i, h)),
        out_shape=jax.ShapeDtypeStruct((batch, seq, GROUP_WIDTH), BF16),
        scratch_shapes=[pltpu.VMEM((seq, LANE), BF16), pltpu.VMEM((seq, 2 * LANE), BF16)],
        compiler_params=_params("parallel", "parallel", "arbitrary"),
        name="fox_attention",
    )(z3, z3, z3, c_col, c_row, gain)


def _l2norm(t):
    return t * lax.rsqrt(jnp.sum(t * t, axis=-1, keepdims=True) + EPS)


def _bmm(a, b):
    return lax.dot_general(a, b, (((2,), (1,)), ((0,), (0,))), preferred_element_type=F32)


def _bmm_nt(a, b):
    return lax.dot_general(a, b, (((2,), (2,)), ((0,), (0,))), preferred_element_type=F32)


def _bmm_tn(a, b):
    return lax.dot_general(a, b, (((1,), (1,)), ((0,), (0,))), preferred_element_type=F32)


def _neumann_tail(p):
    c = p.shape[-1]
    a = p
    q = p
    cover = 2
    while cover < c:
        qb = q.astype(BF16)
        q = _bmm(qb, qb)
        a = a + q + _bmm(a.astype(BF16), q.astype(BF16))
        cover *= 2
    return a


def _gdn_kernel(x_ref, zg_ref, s_ref, cw_ref, aneg_ref, dt_ref, gain_ref, o_ref,
                halo_ref, qkv_ref, state_ref):
    tt = x_ref.shape[0]
    c = GDN_CHUNK
    nc = tt // c
    w = GROUP_WIDTH
    hd = HEAD_DIM

    @pl.when(pl.program_id(1) == 0)
    def _():
        halo_ref[...] = jnp.zeros_like(halo_ref)
        state_ref[...] = jnp.zeros_like(state_ref)

    x = x_ref[...]
    qkv_ref[...] = _silu(_causal_conv(x, halo_ref[...], cw_ref[...], SHORT_CONV))
    halo_ref[...] = x[tt - SUBLANE:, :]
    small = s_ref[...]
    beta_all = _sigmoid(small)
    gc_all = _scan_add(aneg_ref[...] * _softplus(small + dt_ref[...]), c)

    pairs = [(ci, h) for ci in range(nc) for h in range(N_HEADS)]

    def gather(col0):
        return jnp.stack([qkv_ref[ci * c:(ci + 1) * c, col0 + h * hd:col0 + (h + 1) * hd]
                          for ci, h in pairs])

    q = _l2norm(gather(0)) * ATTN_SCALE
    k = _l2norm(gather(w))
    v = gather(2 * w)
    beta = jnp.stack([beta_all[ci * c:(ci + 1) * c, SMALL_BETA + h:SMALL_BETA + h + 1]
                      for ci, h in pairs])
    gc = jnp.stack([gc_all[ci * c:(ci + 1) * c, SMALL_ALPHA + h:SMALL_ALPHA + h + 1]
                    for ci, h in pairs])
    gc_t = [gc_all[ci * c:(ci + 1) * c, :].T for ci in range(nc)]
    gc_row = jnp.stack([gc_t[ci][SMALL_ALPHA + h:SMALL_ALPHA + h + 1, :]
                        for ci, h in pairs])

    row = lax.broadcasted_iota(jnp.int32, (c, c), 0)
    col = lax.broadcasted_iota(jnp.int32, (c, c), 1)
    tril = col <= row
    strict = col < row
    decay = jnp.where(tril, jnp.exp(jnp.where(tril, gc - gc_row, 0.0)), 0.0)
    kb = k.astype(BF16)
    kbeta = k * beta
    kk = _bmm_nt(kbeta.astype(BF16), kb) * decay
    tail = _neumann_tail(jnp.where(strict, -kk, 0.0))
    rhs = jnp.concatenate([v * beta, kbeta * jnp.exp(gc)], axis=-1)
    sol = rhs + _bmm(tail.astype(BF16), rhs.astype(BF16))
    u = sol[:, :, :hd]
    wy = sol[:, :, hd:]
    qk = jnp.where(tril, _bmm_nt(q.astype(BF16), kb) * decay, 0.0).astype(BF16)
    g_last = gc[:, c - 1:c, :]
    k_dec = (k * jnp.exp(g_last - gc)).astype(BF16)
    e_last = jnp.exp(g_last)
    wq = jnp.concatenate([wy.astype(BF16), (q * jnp.exp(gc)).astype(BF16)], axis=1)

    state = state_ref[...]
    for ci in range(nc):
        sl = slice(ci * N_HEADS, (ci + 1) * N_HEADS)
        ws = _bmm(wq[sl], state.astype(BF16))
        v_new = (u[sl] - ws[:, :c, :]).astype(BF16)
        o = ws[:, c:, :] + _bmm(qk[sl], v_new)
        state = state * e_last[sl] + _bmm_tn(k_dec[sl], v_new)
        o = o * lax.rsqrt(jnp.mean(o * o, axis=-1, keepdims=True) + EPS) * gain_ref[...]
        for h in range(N_HEADS):
            zg = zg_ref[ci * c:(ci + 1) * c, h * hd:(h + 1) * hd]
            o_ref[ci * c:(ci + 1) * c, h * hd:(h + 1) * hd] = (o[h] * _silu(zg)).astype(o_ref.dtype)
    state_ref[...] = state


def gdn_call(z, p, batch, seq, tt=512):
    tt = min(tt, seq)
    nt = seq // tt
    w = GROUP_WIDTH
    vec = pl.BlockSpec((1, LANE), lambda b, t: (0, 0))
    return pl.pallas_call(
        _gdn_kernel,
        grid=(batch, nt),
        in_specs=[pl.BlockSpec((tt, 3 * w), lambda b, t: (b * nt + t, COL_CQKV // (3 * w))),
                  pl.BlockSpec((tt, w), lambda b, t: (b * nt + t, COL_CZ // w)),
                  pl.BlockSpec((tt, LANE), lambda b, t: (b * nt + t, COL_SMALL // LANE)),
                  pl.BlockSpec((SHORT_CONV, 3 * w), lambda b, t: (0, 0)),
                  vec, vec, vec],
        out_specs=pl.BlockSpec((tt, w), lambda b, t: (b * nt + t, 0)),
        out_shape=jax.ShapeDtypeStruct((batch * seq, w), BF16),
        scratch_shapes=[pltpu.VMEM((SUBLANE, 3 * w), F32),
                        pltpu.VMEM((tt, 3 * w), F32),
                        pltpu.VMEM((N_HEADS, HEAD_DIM, HEAD_DIM), F32)],
        compiler_params=_params("parallel", "arbitrary"),
        name="gated_deltanet",
    )(z, z, z, p["gdn_conv_w"], p["gdn_aneg"], p["gdn_dt"], p["gdn_norm"])


def _dilated_kernel(q_ref, kc_ref, kp_ref, vc_ref, vp_ref, gain_ref, o_ref,
                    m_ref, acc_ref, l_ref):
    tt = q_ref.shape[0]
    first = pl.program_id(2) == 0
    i = lax.broadcasted_iota(jnp.int32, (SPAN, 2 * SPAN), 0)
    j = lax.broadcasted_iota(jnp.int32, (SPAN, 2 * SPAN), 1)
    band = jnp.logical_and(j >= i, j <= i + SPAN)
    band_first = jnp.logical_and(band, jnp.logical_or(j >= SPAN, jnp.logical_not(first)))
    ones = jnp.ones((2 * SPAN, HEAD_DIM), BF16)

    def rows(ref, start, n, dil):
        return ref[pl.ds(start, n, stride=dil), :] if dil > 1 else ref[pl.ds(start, n), :]

    for g, dil in enumerate(DILATIONS):
        for r in range(dil):
            for sb in range(tt // (dil * SPAN)):
                q0 = r + dil * SPAN * sb
                q = (rows(q_ref, q0, SPAN, dil) * ATTN_SCALE).astype(BF16)
                if sb == 0:
                    p0 = tt - dil * SPAN + r
                    k = jnp.concatenate([rows(kp_ref, p0, SPAN, dil), rows(kc_ref, r, SPAN, dil)], 0)
                    v = jnp.concatenate([rows(vp_ref, p0, SPAN, dil), rows(vc_ref, r, SPAN, dil)], 0)
                else:
                    k = rows(kc_ref, q0 - dil * SPAN, 2 * SPAN, dil)
                    v = rows(vc_ref, q0 - dil * SPAN, 2 * SPAN, dil)
                s = _dot_nt(q, k.astype(BF16))
                s = jnp.where(band_first if sb == 0 else band, s, NEG_INF)
                v1 = jnp.concatenate([v.astype(BF16), ones], axis=1)
                m_blk = jnp.broadcast_to(jnp.max(s, axis=-1, keepdims=True), (SPAN, LANE))
                if g == 0:
                    m_new = m_blk
                else:
                    m_old = rows(m_ref, q0, SPAN, dil)
                    m_new = jnp.maximum(m_old, m_blk)
                p = jnp.exp(s - jnp.concatenate([m_new, m_new], axis=1))
                pv = _dot(p.astype(BF16), v1)
                acc, l = pv[:, :HEAD_DIM], pv[:, HEAD_DIM:]
                if g > 0:
                    a = jnp.exp(m_old - m_new)
                    acc = acc + a * rows(acc_ref, q0, SPAN, dil)
                    l = l + a * rows(l_ref, q0, SPAN, dil)
                dst = pl.ds(q0, SPAN, stride=dil) if dil > 1 else pl.ds(q0, SPAN)
                m_ref[dst, :] = m_new
                acc_ref[dst, :] = acc
                l_ref[dst, :] = l
    y = acc_ref[...] / l_ref[...]
    y = y * lax.rsqrt(jnp.mean(y * y, axis=-1, keepdims=True) + EPS) * gain_ref[...]
    o_ref[...] = y.astype(o_ref.dtype)


def dilated_call(z3, gain, batch, seq):
    tt = max(DILATIONS) * SPAN
    assert seq % tt == 0
    nt = seq // tt
    qb, kb, vb = (COL_DQKV // LANE, (COL_DQKV + GROUP_WIDTH) // LANE,
                  (COL_DQKV + 2 * GROUP_WIDTH) // LANE)

    def cur(off):
        return pl.BlockSpec((None, tt, LANE), lambda b, h, i: (b, i, off + h))

    def prev(off):
        return pl.BlockSpec((None, tt, LANE), lambda b, h, i: (b, jnp.maximum(i - 1, 0), off + h))

    return pl.pallas_call(
        _dilated_kernel,
        grid=(batch, N_HEADS, nt),
        in_specs=[cur(qb), cur(kb), prev(kb), cur(vb), prev(vb),
                  pl.BlockSpec((1, LANE), lambda b, h, i: (0, h))],
        out_specs=pl.BlockSpec((None, tt, LANE), lambda b, h, i: (b, i, h)),
        out_shape=jax.ShapeDtypeStruct((batch, seq, GROUP_WIDTH), BF16),
        scratch_shapes=[pltpu.VMEM((tt, LANE), F32), pltpu.VMEM((tt, HEAD_DIM), F32),
                        pltpu.VMEM((tt, LANE), F32)],
        compiler_params=_params("parallel", "parallel", "arbitrary"),
        name="dilated_attention",
    )(z3, z3, z3, z3, z3, gain)


def _out_proj_kernel(ya, yb, yc, yd, w_ref, x_ref, g_ref, x1_ref, h_ref):
    w = GROUP_WIDTH
    acc = x_ref[...]
    for n, y in enumerate((ya, yb, yc, yd)):
        acc = acc + _dot(y[...], w_ref[n * w:(n + 1) * w, :])
    x1_ref[...] = acc
    h_ref[...] = _rmsnorm_rows(acc, g_ref[...]).astype(h_ref.dtype)


def out_proj_call(ys, w_out, x, gain, tm=512):
    m, d = x.shape
    tm = min(tm, m)
    yblk = pl.BlockSpec((tm, GROUP_WIDTH), lambda i: (i, 0))
    xblk = pl.BlockSpec((tm, d), lambda i: (i, 0))
    return pl.pallas_call(
        _out_proj_kernel,
        grid=(m // tm,),
        in_specs=[yblk] * 4 + [pl.BlockSpec((d, d), lambda i: (0, 0),
                                            pipeline_mode=pl.Buffered(1)), xblk,
                               pl.BlockSpec((1, d), lambda i: (0, 0))],
        out_specs=[xblk, xblk],
        out_shape=[jax.ShapeDtypeStruct((m, d), F32), jax.ShapeDtypeStruct((m, d), BF16)],
        compiler_params=_params("parallel"),
        name="out_proj",
    )(*ys, w_out, x, gain)


FFN_SUB_ROWS = 256


def _ffn_up_kernel(h_ref, wu_ref, wg_ref, cwu_ref, cwg_ref, cbu_ref, cbg_ref, o_ref,
                   tail_u_ref, tail_g_ref, *, tiles_per_seq):
    tm = h_ref.shape[0]

    @pl.when(pl.program_id(1) % tiles_per_seq == 0)
    def _():
        tail_u_ref[...] = jnp.zeros_like(tail_u_ref)
        tail_g_ref[...] = jnp.zeros_like(tail_g_ref)

    sub = min(FFN_SUB_ROWS, tm)
    tail_u = tail_u_ref[...]
    tail_g = tail_g_ref[...]
    for r0 in range(0, tm, sub):
        h = h_ref[r0:r0 + sub, :]
        u = _dot(h, wu_ref[...])
        g = _dot(h, wg_ref[...])
        up = _causal_conv(u, tail_u, cwu_ref[...], FFN_CONV) + cbu_ref[...]
        gate = _causal_conv(g, tail_g, cwg_ref[...], FFN_CONV) + cbg_ref[...]
        tail_u = u[sub - SUBLANE:, :]
        tail_g = g[sub - SUBLANE:, :]
        o_ref[r0:r0 + sub, :] = (_silu(gate) * up).astype(o_ref.dtype)
    tail_u_ref[...] = tail_u
    tail_g_ref[...] = tail_g


def ffn_up_call(h, w_up, conv_w, conv_b, seq, tm=1024, tn=512):
    m, d = h.shape
    tm = min(tm, seq)
    nj = D_FF // tn
    return pl.pallas_call(
        functools.partial(_ffn_up_kernel, tiles_per_seq=seq // tm),
        grid=(nj, m // tm),
        in_specs=[pl.BlockSpec((tm, d), lambda j, i: (i, 0)),
                  pl.BlockSpec((d, tn), lambda j, i: (0, j)),
                  pl.BlockSpec((d, tn), lambda j, i: (0, j + nj)),
                  pl.BlockSpec((FFN_CONV, tn), lambda j, i: (0, j)),
                  pl.BlockSpec((FFN_CONV, tn), lambda j, i: (0, j + nj)),
                  pl.BlockSpec((1, tn), lambda j, i: (0, j)),
                  pl.BlockSpec((1, tn), lambda j, i: (0, j + nj))],
        out_specs=pl.BlockSpec((tm, tn), lambda j, i: (i, j)),
        out_shape=jax.ShapeDtypeStruct((m, D_FF), BF16),
        scratch_shapes=[pltpu.VMEM((SUBLANE, tn), F32), pltpu.VMEM((SUBLANE, tn), F32)],
        compiler_params=_params("parallel", "arbitrary"),
        name="ffn_up",
    )(h, w_up, w_up, conv_w, conv_w, conv_b, conv_b)


def _ffn_down_kernel(g_ref, w_ref, x_ref, gain_ref, x2_ref, hn_ref):
    x2 = x_ref[...] + _dot(g_ref[...], w_ref[...])
    x2_ref[...] = x2
    hn_ref[...] = _rmsnorm_rows(x2, gain_ref[...]).astype(hn_ref.dtype)


def ffn_down_call(g, w_down, x, gain, hn_dtype, tm=256):
    m, d = x.shape
    tm = min(tm, m)
    kdim = g.shape[1]
    xblk = pl.BlockSpec((tm, d), lambda i: (i, 0))
    return pl.pallas_call(
        _ffn_down_kernel,
        grid=(m // tm,),
        in_specs=[pl.BlockSpec((tm, kdim), lambda i: (i, 0)),
                  pl.BlockSpec((kdim, d), lambda i: (0, 0), pipeline_mode=pl.Buffered(1)),
                  xblk, pl.BlockSpec((1, d), lambda i: (0, 0))],
        out_specs=[xblk, xblk],
        out_shape=[jax.ShapeDtypeStruct((m, d), F32), jax.ShapeDtypeStruct((m, d), hn_dtype)],
        compiler_params=_params("parallel"),
        name="ffn_down",
    )(g, w_down, x, gain)


def _block_diag(blocks):
    n, r, c = blocks.shape
    eye = jnp.eye(n, dtype=blocks.dtype)
    return (eye[:, None, :, None] * blocks[:, :, None, :]).reshape(n * r, n * c)


def _small_vec(values, offset):
    return jnp.zeros((1, LANE), F32).at[0, offset:offset + values.shape[0]].set(values)


def _w_in_segments():
    gw = GROUP_WIDTH
    sizes = (gw, gw, 3 * gw, N_HEADS, 3 * gw, gw, N_HEADS, N_HEADS, 3 * gw)
    dsts = (COL_AX, COL_AG, COL_BQKV, COL_SMALL + SMALL_F, COL_CQKV, COL_CZ,
            COL_SMALL + SMALL_BETA, COL_SMALL + SMALL_ALPHA, COL_DQKV)
    segs, src = [], 0
    for size, dst in zip(sizes, dsts):
        segs.append((src, dst, size))
        src += size
    return segs, src


def _prep_w_in_kernel(w_ref, o_ref):
    segs, _ = _w_in_segments()
    o_ref[:, COL_SMALL:] = jnp.zeros((o_ref.shape[0], Z_COLS - COL_SMALL), o_ref.dtype)
    for src, dst, size in segs:
        lo = (src // LANE) * LANE
        hi = min(-(-(src + size) // LANE) * LANE, w_ref.shape[1])
        tile = w_ref[:, lo:hi]
        o_ref[:, dst:dst + size] = tile[:, src - lo:src - lo + size].astype(o_ref.dtype)


def prep_w_in_call(w_in, tk=256):
    depth, k, n = w_in.shape
    assert n == _w_in_segments()[1]
    return pl.pallas_call(
        _prep_w_in_kernel,
        grid=(depth, k // tk),
        in_specs=[pl.BlockSpec((None, tk, n), lambda l, i: (l, i, 0))],
        out_specs=pl.BlockSpec((None, tk, Z_COLS), lambda l, i: (l, i, 0)),
        out_shape=jax.ShapeDtypeStruct((depth, k, Z_COLS), BF16),
        compiler_params=_params("parallel", "parallel"),
        name="prep_w_in",
    )(w_in)


def kernel(x, norm_mix, w_in, lru_conv_w, lru_conv_b, lru_wa, lru_ba, lru_wx, lru_bx,
           lru_lambda, fox_f_bias, gdn_conv_w, gdn_a_log, gdn_dt_bias, gdn_norm,
           norm_a, norm_b, norm_d, w_out, norm_ffn, ffn_w_up, ffn_conv_w, ffn_conv_b,
           ffn_w_down, norm_final):
    batch, seq, d = x.shape
    depth = w_in.shape[0]
    m = batch * seq
    gw = GROUP_WIDTH
    gmat64 = _block_diag(jnp.full((LRU_BLOCKS, LRU_BLOCK_DIM, LRU_BLOCK_DIM),
                                  1.0 / LRU_BLOCK_DIM, F32)).astype(BF16)
    xs = x.reshape(m, d)
    w_in_z = prep_w_in_call(w_in)
    h = rmsnorm_call(xs, norm_mix[0], BF16)
    for l in range(depth):
        p = {
            "lru_conv_w": lru_conv_w[l], "lru_conv_b": lru_conv_b[l].reshape(1, gw),
            "lru_wa": _block_diag(lru_wa[l]).astype(BF16), "lru_ba": lru_ba[l].reshape(1, gw),
            "lru_wx": _block_diag(lru_wx[l]).astype(BF16), "lru_bx": lru_bx[l].reshape(1, gw),
            "lru_lambda": lru_lambda[l].reshape(1, gw), "gmat64": gmat64,
            "norm_a": norm_a[l].reshape(1, gw),
            "gdn_conv_w": gdn_conv_w[l],
            "gdn_aneg": _small_vec(-jnp.exp(gdn_a_log[l]), SMALL_ALPHA),
            "gdn_dt": _small_vec(gdn_dt_bias[l], SMALL_ALPHA),
            "gdn_norm": gdn_norm[l].reshape(1, HEAD_DIM),
        }
        z = in_proj_call(h, w_in_z[l])

        y_a = lru_call(z, p, batch, seq)

        c = logf_cumsum_call(z, _small_vec(fox_f_bias[l], SMALL_F), batch, seq)
        c = c.reshape(batch, seq, LANE)[:, :, SMALL_F:SMALL_F + N_HEADS]
        c = jnp.transpose(c, (0, 2, 1))
        z3 = z.reshape(batch, seq, Z_COLS)
        y_b = fox_call(z3, c, norm_b[l].reshape(1, gw), batch, seq).reshape(m, gw)

        y_c = gdn_call(z, p, batch, seq)

        y_d = dilated_call(z3, norm_d[l].reshape(1, gw), batch, seq).reshape(m, gw)

        x1, hf = out_proj_call((y_a, y_b, y_c, y_d), w_out[l].astype(BF16), xs,
                               norm_ffn[l].reshape(1, d))
        g = ffn_up_call(hf, ffn_w_up[l].astype(BF16), ffn_conv_w[l],
                        ffn_conv_b[l].reshape(1, 2 * D_FF), seq)
        last = l == depth - 1
        gain_next = norm_final if last else norm_mix[l + 1]
        xs, h = ffn_down_call(g, ffn_w_down[l].astype(BF16), x1, gain_next.reshape(1, d),
                              F32 if last else BF16)
    return h.reshape(batch, seq, d)
```

```python
import functools
import math

import jax
import jax.numpy as jnp
from jax import lax
from jax.experimental import pallas as pl
from jax.experimental.pallas import tpu as pltpu

F32 = jnp.float32
BF16 = jnp.bfloat16

D_MODEL = 2048
GROUP_WIDTH = 512
HEAD_DIM = 128
N_HEADS = 4
LRU_BLOCKS = 8
LRU_BLOCK_DIM = 64
LRU_C = 8.0
SHORT_CONV = 4
FFN_CONV = 3
D_FF = 5632
GDN_CHUNK = 64
SPAN = 128
DILATIONS = (1, 4, 16)
EPS = 1e-6
NEG_INF = -1e30
ATTN_SCALE = HEAD_DIM ** -0.5

LANE = 128
SUBLANE = 8
VMEM_LIMIT = 52 * 1024 * 1024

COL_CQKV = 0
COL_AX = 1536
COL_AG = 2048
COL_CZ = 2560
COL_BQKV = 3072
COL_DQKV = 4608
COL_SMALL = 6144
Z_COLS = 6400
SMALL_F, SMALL_BETA, SMALL_ALPHA = 0, 4, 8


def _params(*sem):
    return pltpu.CompilerParams(dimension_semantics=sem, vmem_limit_bytes=VMEM_LIMIT)


def _sigmoid(x):
    return 1.0 / (1.0 + jnp.exp(-x))


def _softplus(x):
    return jnp.maximum(x, 0.0) + jnp.log1p(jnp.exp(-jnp.abs(x)))


def _silu(x):
    return x * _sigmoid(x)


def _gelu_tanh(x):
    c = math.sqrt(2.0 / math.pi)
    return 0.5 * x * (1.0 + jnp.tanh(c * (x + 0.044715 * (x * x * x))))


def _dot(a, b, precision=None):
    return jnp.dot(a, b, preferred_element_type=F32, precision=precision)


def _dot_nt(a, b):
    return lax.dot_general(a, b, (((1,), (1,)), ((), ())), preferred_element_type=F32)


def _dot_tn(a, b):
    return lax.dot_general(a, b, (((0,), (0,)), ((), ())), preferred_element_type=F32)


def _causal_conv(x, halo, w, taps):
    n = x.shape[0]
    cat = jnp.concatenate([halo, x], axis=0)
    y = w[taps - 1:taps, :] * x
    for k in range(taps - 1):
        shifted = pltpu.roll(cat, taps - 1 - k, 0)[SUBLANE:SUBLANE + n, :]
        y = y + w[k:k + 1, :] * shifted
    return y


def _scan_affine(a, u):
    n = a.shape[0]
    row = lax.broadcasted_iota(jnp.int32, a.shape, 0)
    s = 1
    while s < n:
        keep = row >= s
        a_s = jnp.where(keep, pltpu.roll(a, s, 0), 1.0)
        u_s = jnp.where(keep, pltpu.roll(u, s, 0), 0.0)
        u = a * u_s + u
        a = a * a_s
        s *= 2
    return a, u


def _scan_add(x, seg):
    row = lax.broadcasted_iota(jnp.int32, x.shape, 0)
    pos = row % seg
    s = 1
    while s < seg:
        x = x + jnp.where(pos >= s, pltpu.roll(x, s, 0), 0.0)
        s *= 2
    return x


def _rmsnorm_rows(x, gain):
    ms = jnp.mean(x * x, axis=-1, keepdims=True)
    return x * lax.rsqrt(ms + EPS) * gain


def _rmsnorm_kernel(x_ref, g_ref, o_ref):
    o_ref[...] = _rmsnorm_rows(x_ref[...], g_ref[...]).astype(o_ref.dtype)


def rmsnorm_call(x, gain, out_dtype, tm=512):
    m, d = x.shape
    return pl.pallas_call(
        _rmsnorm_kernel,
        grid=(m // tm,),
        in_specs=[pl.BlockSpec((tm, d), lambda i: (i, 0)),
                  pl.BlockSpec((1, d), lambda i: (0, 0))],
        out_specs=pl.BlockSpec((tm, d), lambda i: (i, 0)),
        out_shape=jax.ShapeDtypeStruct((m, d), out_dtype),
        compiler_params=_params("parallel"),
        name="rmsnorm",
    )(x, gain.reshape(1, d))


def _matmul_kernel(a_ref, b_ref, o_ref):
    o_ref[...] = _dot(a_ref[...], b_ref[...])


def in_proj_call(h, w, layer, tm=1024, tn=1280):
    m, k = h.shape
    n = w.shape[2]
    tm = min(tm, m)
    return pl.pallas_call(
        _matmul_kernel,
        grid=(m // tm, n // tn),
        in_specs=[pl.BlockSpec((tm, k), lambda i, j: (i, 0)),
                  pl.BlockSpec((None, k, tn), lambda i, j: (layer, 0, j))],
        out_specs=pl.BlockSpec((tm, tn), lambda i, j: (i, j)),
        out_shape=jax.ShapeDtypeStruct((m, n), F32),
        compiler_params=_params("parallel", "arbitrary"),
        name="in_proj",
    )(h, w)


def _group_mean_sq(h, gmat):
    h2 = h * h
    hi = h2.astype(BF16)
    lo = (h2 - hi.astype(F32)).astype(BF16)
    return _dot(hi, gmat) + _dot(lo, gmat)


def _lru_kernel(ax_ref, ag_ref, cw_ref, cb_ref, wa_ref, ba_ref, wx_ref, bx_ref, lam_ref,
                gmat_ref, gain_ref, o_ref, halo_ref, hprev_ref):
    tt = ax_ref.shape[0]

    @pl.when(pl.program_id(1) == 0)
    def _():
        halo_ref[...] = jnp.zeros_like(halo_ref)
        hprev_ref[...] = jnp.zeros_like(hprev_ref)

    x = ax_ref[...]
    xc = _causal_conv(x, halo_ref[...], cw_ref[...], SHORT_CONV) + cb_ref[...]
    halo_ref[...] = x[tt - SUBLANE:, :]
    xb = xc.astype(BF16)
    r = _sigmoid(_dot(xb, wa_ref[...]) + ba_ref[...])
    i = _sigmoid(_dot(xb, wx_ref[...]) + bx_ref[...])
    log_a = (-LRU_C) * r * _softplus(-lam_ref[...])
    a = jnp.exp(log_a)
    u = jnp.sqrt(-jnp.tanh(log_a) * (a * a + 1.0)) * (i * xc)
    a_cum, h = _scan_affine(a, u)
    h = h + a_cum * hprev_ref[...]
    hprev_ref[...] = h[tt - 1:tt, :]
    y = h * lax.rsqrt(_group_mean_sq(h, gmat_ref[...]) + EPS) * gain_ref[...]
    o_ref[...] = (y * _gelu_tanh(ag_ref[...])).astype(o_ref.dtype)


def lru_call(z, p, batch, seq, tt=512):
    tt = min(tt, seq)
    nt = seq // tt
    w = GROUP_WIDTH
    row = lambda b, t: (b * nt + t)
    vec = pl.BlockSpec((1, w), lambda b, t: (0, 0))
    mat = pl.BlockSpec((w, w), lambda b, t: (0, 0))
    return pl.pallas_call(
        _lru_kernel,
        grid=(batch, nt),
        in_specs=[pl.BlockSpec((tt, w), lambda b, t: (row(b, t), COL_AX // w)),
                  pl.BlockSpec((tt, w), lambda b, t: (row(b, t), COL_AG // w)),
                  pl.BlockSpec((SHORT_CONV, w), lambda b, t: (0, 0)),
                  vec, mat, vec, mat, vec, vec, mat, vec],
        out_specs=pl.BlockSpec((tt, w), lambda b, t: (row(b, t), 0)),
        out_shape=jax.ShapeDtypeStruct((batch * seq, w), BF16),
        scratch_shapes=[pltpu.VMEM((SUBLANE, w), F32), pltpu.VMEM((1, w), F32)],
        compiler_params=_params("parallel", "arbitrary"),
        name="rg_lru",
    )(z, z, p["lru_conv_w"], p["lru_conv_b"], p["lru_wa"], p["lru_ba"], p["lru_wx"],
      p["lru_bx"], p["lru_lambda"], p["gmat64"], p["norm_a"])


def _logf_cumsum_kernel(s_ref, bias_ref, o_ref, carry_ref):
    tt = s_ref.shape[0]

    @pl.when(pl.program_id(1) == 0)
    def _():
        carry_ref[...] = jnp.zeros_like(carry_ref)

    log_f = -_softplus(-(s_ref[...] + bias_ref[...]))
    c = _scan_add(log_f, tt) + carry_ref[...]
    carry_ref[...] = c[tt - 1:tt, :]
    o_ref[...] = c


def logf_cumsum_call(z, bias, batch, seq, tt=1024):
    tt = min(tt, seq)
    nt = seq // tt
    return pl.pallas_call(
        _logf_cumsum_kernel,
        grid=(batch, nt),
        in_specs=[pl.BlockSpec((tt, LANE), lambda b, t: (b * nt + t, COL_SMALL // LANE)),
                  pl.BlockSpec((1, LANE), lambda b, t: (0, 0))],
        out_specs=pl.BlockSpec((tt, LANE), lambda b, t: (b * nt + t, 0)),
        out_shape=jax.ShapeDtypeStruct((batch * seq, LANE), F32),
        scratch_shapes=[pltpu.VMEM((1, LANE), F32)],
        compiler_params=_params("parallel", "arbitrary"),
        name="logf_cumsum",
    )(z, bias)


FOX_ROW_CHUNK = 64


def _fox_kernel(q_ref, k_ref, v_ref, cq_ref, ck_ref, gain_ref, o_ref, kb_ref, vb_ref,
                s_s, p_s, m_s, alpha_s, cq_s, acc_s, *, tq, tk):
    qi = pl.program_id(2)

    @pl.when(qi == 0)
    def _():
        kb_ref[...] = k_ref[...].astype(BF16)
        vb_ref[:, :HEAD_DIM] = v_ref[...].astype(BF16)
        vb_ref[:, HEAD_DIM:] = jnp.ones((vb_ref.shape[0], HEAD_DIM), BF16)

    q = (q_ref[...] * ATTN_SCALE).astype(BF16)
    cq_s[...] = jnp.broadcast_to(cq_ref[...], (tq, LANE))
    m_s[...] = jnp.full((tq, LANE), NEG_INF, F32)
    acc_s[...] = jnp.zeros((tq, 2 * HEAD_DIM), F32)
    rc = min(FOX_ROW_CHUNK, tq)

    def lanes(x, n):
        return jnp.concatenate([x] * (n // LANE), axis=1)

    def scores(kv, slot):
        start = pl.multiple_of(kv * tk, tk)
        s_s[slot] = _dot_nt(q, kb_ref[pl.ds(start, tk), :]) - ck_ref[kv]

    def update(kv, slot, diagonal):
        for r0 in range(0, tq, rc):
            rows = slice(r0, r0 + rc)
            s = s_s[slot, rows, :]
            if diagonal:
                row = lax.broadcasted_iota(jnp.int32, s.shape, 0) + r0
                col = lax.broadcasted_iota(jnp.int32, s.shape, 1)
                s = jnp.where(col <= row, s, NEG_INF)
            m_old = m_s[rows, :]
            m_new = jnp.maximum(m_old, jnp.max(s, axis=-1, keepdims=True) + cq_s[rows, :])
            p_s[rows, :] = jnp.exp(s - lanes(m_new - cq_s[rows, :], tk)).astype(BF16)
            alpha_s[rows, :] = jnp.exp(m_old - m_new)
            m_s[rows, :] = m_new
        start = pl.multiple_of(kv * tk, tk)
        pv = _dot(p_s[...], vb_ref[pl.ds(start, tk), :])
        acc_s[...] = lanes(alpha_s[...], 2 * HEAD_DIM) * acc_s[...] + pv

    def pair(jj, carry):
        scores(2 * jj + 1, 1)
        update(2 * jj, 0, False)
        scores(2 * jj + 2, 0)
        update(2 * jj + 1, 1, False)
        return carry

    scores(0, 0)
    lax.fori_loop(0, qi // 2, pair, 0)

    @pl.when(qi % 2 == 1)
    def _():
        scores(qi, 1)
        update(qi - 1, 0, False)
        update(qi, 1, True)

    @pl.when(qi % 2 == 0)
    def _():
        update(qi, 0, True)

    acc = acc_s[...]
    o = acc[:, :HEAD_DIM] / acc[:, HEAD_DIM:]
    o = o * lax.rsqrt(jnp.mean(o * o, axis=-1, keepdims=True) + EPS) * gain_ref[...]
    o_ref[...] = o.astype(o_ref.dtype)


def fox_call(z3, c, gain, batch, seq, tq=512):
    tq = min(tq, seq)
    tk = tq
    nq = seq // tq
    nk = seq // tk
    qb, kb, vb = (COL_BQKV // LANE, (COL_BQKV + GROUP_WIDTH) // LANE,
                  (COL_BQKV + 2 * GROUP_WIDTH) // LANE)
    c_col = c.reshape(batch, N_HEADS, seq, 1)
    c_row = c.reshape(batch, N_HEADS, nk, 1, tk)
    return pl.pallas_call(
        functools.partial(_fox_kernel, tq=tq, tk=tk),
        grid=(batch, N_HEADS, nq),
        in_specs=[pl.BlockSpec((None, tq, LANE), lambda b, h, i: (b, i, qb + h)),
                  pl.BlockSpec((None, seq, LANE), lambda b, h, i: (b, 0, kb + h)),
                  pl.BlockSpec((None, seq, LANE), lambda b, h, i: (b, 0, vb + h)),
                  pl.BlockSpec((None, None, tq, 1), lambda b, h, i: (b, h, i, 0)),
                  pl.BlockSpec((None, None, nk, 1, tk), lambda b, h, i: (b, h, 0, 0, 0)),
                  pl.BlockSpec((1, LANE), lambda b, h, i: (0, h))],
        out_specs=pl.BlockSpec((None, tq, LANE), lambda b, h, i: (b, i, h)),
        out_shape=jax.ShapeDtypeStruct((batch, seq, GROUP_WIDTH), BF16),
        scratch_shapes=[pltpu.VMEM((seq, LANE), BF16), pltpu.VMEM((seq, 2 * LANE), BF16),
                        pltpu.VMEM((2, tq, tk), F32), pltpu.VMEM((tq, tk), BF16),
                        pltpu.VMEM((tq, LANE), F32), pltpu.VMEM((tq, LANE), F32),
                        pltpu.VMEM((tq, LANE), F32), pltpu.VMEM((tq, 2 * HEAD_DIM), F32)],
        compiler_params=_params("parallel", "parallel", "arbitrary"),
        name="fox_attention",
    )(z3, z3, z3, c_col, c_row, gain)


def _l2norm(t):
    return t * lax.rsqrt(jnp.sum(t * t, axis=-1, keepdims=True) + EPS)


def _bmm(a, b):
    return lax.dot_general(a, b, (((2,), (1,)), ((0,), (0,))), preferred_element_type=F32)


def _bmm_nt(a, b):
    return lax.dot_general(a, b, (((2,), (2,)), ((0,), (0,))), preferred_element_type=F32)


def _bmm_tn(a, b):
    return lax.dot_general(a, b, (((1,), (1,)), ((0,), (0,))), preferred_element_type=F32)


def _neumann_tail(p):
    c = p.shape[-1]
    a = p
    q = p
    cover = 2
    while cover < c:
        qb = q.astype(BF16)
        q = _bmm(qb, qb)
        a = a + q + _bmm(a.astype(BF16), q.astype(BF16))
        cover *= 2
    return a


def _gdn_kernel(x_ref, zg_ref, s_ref, cw_ref, aneg_ref, dt_ref, gain_ref, o_ref,
                halo_ref, qkv_ref, state_ref):
    tt = x_ref.shape[0]
    c = GDN_CHUNK
    nc = tt // c
    w = GROUP_WIDTH
    hd = HEAD_DIM

    @pl.when(pl.program_id(1) == 0)
    def _():
        halo_ref[...] = jnp.zeros_like(halo_ref)
        state_ref[...] = jnp.zeros_like(state_ref)

    x = x_ref[...]
    qkv_ref[...] = _silu(_causal_conv(x, halo_ref[...], cw_ref[...], SHORT_CONV))
    halo_ref[...] = x[tt - SUBLANE:, :]
    small = s_ref[...]
    beta_all = _sigmoid(small)
    gc_all = _scan_add(aneg_ref[...] * _softplus(small + dt_ref[...]), c)

    pairs = [(ci, h) for ci in range(nc) for h in range(N_HEADS)]

    def gather(col0):
        return jnp.stack([qkv_ref[ci * c:(ci + 1) * c, col0 + h * hd:col0 + (h + 1) * hd]
                          for ci, h in pairs])

    q = _l2norm(gather(0)) * ATTN_SCALE
    k = _l2norm(gather(w))
    v = gather(2 * w)
    beta = jnp.stack([beta_all[ci * c:(ci + 1) * c, SMALL_BETA + h:SMALL_BETA + h + 1]
                      for ci, h in pairs])
    gc = jnp.stack([gc_all[ci * c:(ci + 1) * c, SMALL_ALPHA + h:SMALL_ALPHA + h + 1]
                    for ci, h in pairs])
    gc_t = [gc_all[ci * c:(ci + 1) * c, :].T for ci in range(nc)]
    gc_row = jnp.stack([gc_t[ci][SMALL_ALPHA + h:SMALL_ALPHA + h + 1, :]
                        for ci, h in pairs])

    row = lax.broadcasted_iota(jnp.int32, (c, c), 0)
    col = lax.broadcasted_iota(jnp.int32, (c, c), 1)
    tril = col <= row
    strict = col < row
    decay = jnp.where(tril, jnp.exp(jnp.where(tril, gc - gc_row, 0.0)), 0.0)
    kb = k.astype(BF16)
    kbeta = k * beta
    kk = _bmm_nt(kbeta.astype(BF16), kb) * decay
    tail = _neumann_tail(jnp.where(strict, -kk, 0.0))
    rhs = jnp.concatenate([v * beta, kbeta * jnp.exp(gc)], axis=-1)
    sol = rhs + _bmm(tail.astype(BF16), rhs.astype(BF16))
    u = sol[:, :, :hd]
    wy = sol[:, :, hd:]
    qk = jnp.where(tril, _bmm_nt(q.astype(BF16), kb) * decay, 0.0).astype(BF16)
    g_last = gc[:, c - 1:c, :]
    k_dec = (k * jnp.exp(g_last - gc)).astype(BF16)
    e_last = jnp.exp(g_last)
    wq = jnp.concatenate([wy.astype(BF16), (q * jnp.exp(gc)).astype(BF16)], axis=1)

    state = state_ref[...]
    for ci in range(nc):
        sl = slice(ci * N_HEADS, (ci + 1) * N_HEADS)
        ws = _bmm(wq[sl], state.astype(BF16))
        v_new = (u[sl] - ws[:, :c, :]).astype(BF16)
        o = ws[:, c:, :] + _bmm(qk[sl], v_new)
        state = state * e_last[sl] + _bmm_tn(k_dec[sl], v_new)
        o = o * lax.rsqrt(jnp.mean(o * o, axis=-1, keepdims=True) + EPS) * gain_ref[...]
        for h in range(N_HEADS):
            zg = zg_ref[ci * c:(ci + 1) * c, h * hd:(h + 1) * hd]
            o_ref[ci * c:(ci + 1) * c, h * hd:(h + 1) * hd] = (o[h] * _silu(zg)).astype(o_ref.dtype)
    state_ref[...] = state


def gdn_call(z, p, batch, seq, tt=512):
    tt = min(tt, seq)
    nt = seq // tt
    w = GROUP_WIDTH
    vec = pl.BlockSpec((1, LANE), lambda b, t: (0, 0))
    return pl.pallas_call(
        _gdn_kernel,
        grid=(batch, nt),
        in_specs=[pl.BlockSpec((tt, 3 * w), lambda b, t: (b * nt + t, COL_CQKV // (3 * w))),
                  pl.BlockSpec((tt, w), lambda b, t: (b * nt + t, COL_CZ // w)),
                  pl.BlockSpec((tt, LANE), lambda b, t: (b * nt + t, COL_SMALL // LANE)),
                  pl.BlockSpec((SHORT_CONV, 3 * w), lambda b, t: (0, 0)),
                  vec, vec, vec],
        out_specs=pl.BlockSpec((tt, w), lambda b, t: (b * nt + t, 0)),
        out_shape=jax.ShapeDtypeStruct((batch * seq, w), BF16),
        scratch_shapes=[pltpu.VMEM((SUBLANE, 3 * w), F32),
                        pltpu.VMEM((tt, 3 * w), F32),
                        pltpu.VMEM((N_HEADS, HEAD_DIM, HEAD_DIM), F32)],
        compiler_params=_params("parallel", "arbitrary"),
        name="gated_deltanet",
    )(z, z, z, p["gdn_conv_w"], p["gdn_aneg"], p["gdn_dt"], p["gdn_norm"])


def _dilated_kernel(q_ref, kc_ref, kp_ref, vc_ref, vp_ref, gain_ref, o_ref,
                    m_ref, acc_ref, l_ref):
    tt = q_ref.shape[0]
    first = pl.program_id(2) == 0
    i = lax.broadcasted_iota(jnp.int32, (SPAN, 2 * SPAN), 0)
    j = lax.broadcasted_iota(jnp.int32, (SPAN, 2 * SPAN), 1)
    band = jnp.logical_and(j >= i, j <= i + SPAN)
    band_first = jnp.logical_and(band, jnp.logical_or(j >= SPAN, jnp.logical_not(first)))
    ones = jnp.ones((2 * SPAN, HEAD_DIM), BF16)

    def rows(ref, start, n, dil):
        return ref[pl.ds(start, n, stride=dil), :] if dil > 1 else ref[pl.ds(start, n), :]

    for g, dil in enumerate(DILATIONS):
        for r in range(dil):
            for sb in range(tt // (dil * SPAN)):
                q0 = r + dil * SPAN * sb
                q = (rows(q_ref, q0, SPAN, dil) * ATTN_SCALE).astype(BF16)
                if sb == 0:
                    p0 = tt - dil * SPAN + r
                    k = jnp.concatenate([rows(kp_ref, p0, SPAN, dil), rows(kc_ref, r, SPAN, dil)], 0)
                    v = jnp.concatenate([rows(vp_ref, p0, SPAN, dil), rows(vc_ref, r, SPAN, dil)], 0)
                else:
                    k = rows(kc_ref, q0 - dil * SPAN, 2 * SPAN, dil)
                    v = rows(vc_ref, q0 - dil * SPAN, 2 * SPAN, dil)
                s = _dot_nt(q, k.astype(BF16))
                s = jnp.where(band_first if sb == 0 else band, s, NEG_INF)
                v1 = jnp.concatenate([v.astype(BF16), ones], axis=1)
                m_blk = jnp.broadcast_to(jnp.max(s, axis=-1, keepdims=True), (SPAN, LANE))
                if g == 0:
                    m_new = m_blk
                else:
                    m_old = rows(m_ref, q0, SPAN, dil)
                    m_new = jnp.maximum(m_old, m_blk)
                p = jnp.exp(s - jnp.concatenate([m_new, m_new], axis=1))
                pv = _dot(p.astype(BF16), v1)
                acc, l = pv[:, :HEAD_DIM], pv[:, HEAD_DIM:]
                if g > 0:
                    a = jnp.exp(m_old - m_new)
                    acc = acc + a * rows(acc_ref, q0, SPAN, dil)
                    l = l + a * rows(l_ref, q0, SPAN, dil)
                dst = pl.ds(q0, SPAN, stride=dil) if dil > 1 else pl.ds(q0, SPAN)
                m_ref[dst, :] = m_new
                acc_ref[dst, :] = acc
                l_ref[dst, :] = l
    y = acc_ref[...] / l_ref[...]
    y = y * lax.rsqrt(jnp.mean(y * y, axis=-1, keepdims=True) + EPS) * gain_ref[...]
    o_ref[...] = y.astype(o_ref.dtype)


def dilated_call(z3, gain, batch, seq):
    tt = max(DILATIONS) * SPAN
    assert seq % tt == 0
    nt = seq // tt
    qb, kb, vb = (COL_DQKV // LANE, (COL_DQKV + GROUP_WIDTH) // LANE,
                  (COL_DQKV + 2 * GROUP_WIDTH) // LANE)

    def cur(off):
        return pl.BlockSpec((None, tt, LANE), lambda b, h, i: (b, i, off + h))

    def prev(off):
        return pl.BlockSpec((None, tt, LANE), lambda b, h, i: (b, jnp.maximum(i - 1, 0), off + h))

    return pl.pallas_call(
        _dilated_kernel,
        grid=(batch, N_HEADS, nt),
        in_specs=[cur(qb), cur(kb), prev(kb), cur(vb), prev(vb),
                  pl.BlockSpec((1, LANE), lambda b, h, i: (0, h))],
        out_specs=pl.BlockSpec((None, tt, LANE), lambda b, h, i: (b, i, h)),
        out_shape=jax.ShapeDtypeStruct((batch, seq, GROUP_WIDTH), BF16),
        scratch_shapes=[pltpu.VMEM((tt, LANE), F32), pltpu.VMEM((tt, HEAD_DIM), F32),
                        pltpu.VMEM((tt, LANE), F32)],
        compiler_params=_params("parallel", "parallel", "arbitrary"),
        name="dilated_attention",
    )(z3, z3, z3, z3, z3, gain)


def _out_proj_kernel(ya, yb, yc, yd, w_ref, x_ref, g_ref, x1_ref, h_ref):
    w = GROUP_WIDTH
    acc = x_ref[...]
    for n, y in enumerate((ya, yb, yc, yd)):
        acc = acc + _dot(y[...], w_ref[n * w:(n + 1) * w, :])
    x1_ref[...] = acc
    h_ref[...] = _rmsnorm_rows(acc, g_ref[...]).astype(h_ref.dtype)


def out_proj_call(ys, w_out, layer, x, gain, tm=512):
    m, d = x.shape
    tm = min(tm, m)
    yblk = pl.BlockSpec((tm, GROUP_WIDTH), lambda i: (i, 0))
    xblk = pl.BlockSpec((tm, d), lambda i: (i, 0))
    return pl.pallas_call(
        _out_proj_kernel,
        grid=(m // tm,),
        in_specs=[yblk] * 4 + [pl.BlockSpec((None, d, d), lambda i: (layer, 0, 0),
                                            pipeline_mode=pl.Buffered(1)), xblk,
                               pl.BlockSpec((1, d), lambda i: (0, 0))],
        out_specs=[xblk, xblk],
        out_shape=[jax.ShapeDtypeStruct((m, d), F32), jax.ShapeDtypeStruct((m, d), BF16)],
        compiler_params=_params("parallel"),
        name="out_proj",
    )(*ys, w_out, x, gain)


FFN_SUB_ROWS = 256


def _ffn_up_kernel(h_ref, wu_ref, wg_ref, cwu_ref, cwg_ref, cbu_ref, cbg_ref, o_ref,
                   tail_u_ref, tail_g_ref, *, tiles_per_seq):
    tm = h_ref.shape[0]

    @pl.when(pl.program_id(1) % tiles_per_seq == 0)
    def _():
        tail_u_ref[...] = jnp.zeros_like(tail_u_ref)
        tail_g_ref[...] = jnp.zeros_like(tail_g_ref)

    sub = min(FFN_SUB_ROWS, tm)
    tail_u = tail_u_ref[...]
    tail_g = tail_g_ref[...]
    for r0 in range(0, tm, sub):
        h = h_ref[r0:r0 + sub, :]
        u = _dot(h, wu_ref[...])
        g = _dot(h, wg_ref[...])
        up = _causal_conv(u, tail_u, cwu_ref[...], FFN_CONV) + cbu_ref[...]
        gate = _causal_conv(g, tail_g, cwg_ref[...], FFN_CONV) + cbg_ref[...]
        tail_u = u[sub - SUBLANE:, :]
        tail_g = g[sub - SUBLANE:, :]
        o_ref[r0:r0 + sub, :] = (_silu(gate) * up).astype(o_ref.dtype)
    tail_u_ref[...] = tail_u
    tail_g_ref[...] = tail_g


def ffn_up_call(h, w_up, layer, conv_w, conv_b, seq, tm=1024, tn=512):
    m, d = h.shape
    tm = min(tm, seq)
    nj = D_FF // tn
    return pl.pallas_call(
        functools.partial(_ffn_up_kernel, tiles_per_seq=seq // tm),
        grid=(nj, m // tm),
        in_specs=[pl.BlockSpec((tm, d), lambda j, i: (i, 0)),
                  pl.BlockSpec((None, d, tn), lambda j, i: (layer, 0, j)),
                  pl.BlockSpec((None, d, tn), lambda j, i: (layer, 0, j + nj)),
                  pl.BlockSpec((FFN_CONV, tn), lambda j, i: (0, j)),
                  pl.BlockSpec((FFN_CONV, tn), lambda j, i: (0, j + nj)),
                  pl.BlockSpec((1, tn), lambda j, i: (0, j)),
                  pl.BlockSpec((1, tn), lambda j, i: (0, j + nj))],
        out_specs=pl.BlockSpec((tm, tn), lambda j, i: (i, j)),
        out_shape=jax.ShapeDtypeStruct((m, D_FF), BF16),
        scratch_shapes=[pltpu.VMEM((SUBLANE, tn), F32), pltpu.VMEM((SUBLANE, tn), F32)],
        compiler_params=_params("parallel", "arbitrary"),
        name="ffn_up",
    )(h, w_up, w_up, conv_w, conv_w, conv_b, conv_b)


def _ffn_down_kernel(g_ref, w_ref, x_ref, gain_ref, x2_ref, hn_ref):
    x2 = x_ref[...] + _dot(g_ref[...], w_ref[...])
    x2_ref[...] = x2
    hn_ref[...] = _rmsnorm_rows(x2, gain_ref[...]).astype(hn_ref.dtype)


def ffn_down_call(g, w_down, layer, x, gain, hn_dtype, tm=256):
    m, d = x.shape
    tm = min(tm, m)
    kdim = g.shape[1]
    xblk = pl.BlockSpec((tm, d), lambda i: (i, 0))
    return pl.pallas_call(
        _ffn_down_kernel,
        grid=(m // tm,),
        in_specs=[pl.BlockSpec((tm, kdim), lambda i: (i, 0)),
                  pl.BlockSpec((None, kdim, d), lambda i: (layer, 0, 0),
                               pipeline_mode=pl.Buffered(1)),
                  xblk, pl.BlockSpec((1, d), lambda i: (0, 0))],
        out_specs=[xblk, xblk],
        out_shape=[jax.ShapeDtypeStruct((m, d), F32), jax.ShapeDtypeStruct((m, d), hn_dtype)],
        compiler_params=_params("parallel"),
        name="ffn_down",
    )(g, w_down, x, gain)


def _block_diag(blocks):
    n, r, c = blocks.shape
    eye = jnp.eye(n, dtype=blocks.dtype)
    return (eye[:, None, :, None] * blocks[:, :, None, :]).reshape(n * r, n * c)


def _small_vec(values, offset):
    return jnp.zeros((1, LANE), F32).at[0, offset:offset + values.shape[0]].set(values)


def _w_in_segments():
    gw = GROUP_WIDTH
    sizes = (gw, gw, 3 * gw, N_HEADS, 3 * gw, gw, N_HEADS, N_HEADS, 3 * gw)
    dsts = (COL_AX, COL_AG, COL_BQKV, COL_SMALL + SMALL_F, COL_CQKV, COL_CZ,
            COL_SMALL + SMALL_BETA, COL_SMALL + SMALL_ALPHA, COL_DQKV)
    segs, src = [], 0
    for size, dst in zip(sizes, dsts):
        segs.append((src, dst, size))
        src += size
    return segs, src


def _prep_w_in_kernel(w_ref, o_ref):
    segs, _ = _w_in_segments()
    o_ref[:, COL_SMALL:] = jnp.zeros((o_ref.shape[0], Z_COLS - COL_SMALL), o_ref.dtype)
    for src, dst, size in segs:
        lo = (src // LANE) * LANE
        hi = min(-(-(src + size) // LANE) * LANE, w_ref.shape[1])
        tile = w_ref[:, lo:hi]
        o_ref[:, dst:dst + size] = tile[:, src - lo:src - lo + size].astype(o_ref.dtype)


def prep_w_in_call(w_in, tk=256):
    depth, k, n = w_in.shape
    assert n == _w_in_segments()[1]
    return pl.pallas_call(
        _prep_w_in_kernel,
        grid=(depth, k // tk),
        in_specs=[pl.BlockSpec((None, tk, n), lambda l, i: (l, i, 0))],
        out_specs=pl.BlockSpec((None, tk, Z_COLS), lambda l, i: (l, i, 0)),
        out_shape=jax.ShapeDtypeStruct((depth, k, Z_COLS), BF16),
        compiler_params=_params("parallel", "parallel"),
        name="prep_w_in",
    )(w_in)


def kernel(x, norm_mix, w_in, lru_conv_w, lru_conv_b, lru_wa, lru_ba, lru_wx, lru_bx,
           lru_lambda, fox_f_bias, gdn_conv_w, gdn_a_log, gdn_dt_bias, gdn_norm,
           norm_a, norm_b, norm_d, w_out, norm_ffn, ffn_w_up, ffn_conv_w, ffn_conv_b,
           ffn_w_down, norm_final):
    batch, seq, d = x.shape
    depth = w_in.shape[0]
    m = batch * seq
    gw = GROUP_WIDTH
    gmat64 = _block_diag(jnp.full((LRU_BLOCKS, LRU_BLOCK_DIM, LRU_BLOCK_DIM),
                                  1.0 / LRU_BLOCK_DIM, F32)).astype(BF16)
    xs = x.reshape(m, d)
    w_in_z = prep_w_in_call(w_in)
    w_out_b = w_out.astype(BF16)
    w_up_b = ffn_w_up.astype(BF16)
    w_down_b = ffn_w_down.astype(BF16)
    h = rmsnorm_call(xs, norm_mix[0], BF16)
    for l in range(depth):
        p = {
            "lru_conv_w": lru_conv_w[l], "lru_conv_b": lru_conv_b[l].reshape(1, gw),
            "lru_wa": _block_diag(lru_wa[l]).astype(BF16), "lru_ba": lru_ba[l].reshape(1, gw),
            "lru_wx": _block_diag(lru_wx[l]).astype(BF16), "lru_bx": lru_bx[l].reshape(1, gw),
            "lru_lambda": lru_lambda[l].reshape(1, gw), "gmat64": gmat64,
            "norm_a": norm_a[l].reshape(1, gw),
            "gdn_conv_w": gdn_conv_w[l],
            "gdn_aneg": _small_vec(-jnp.exp(gdn_a_log[l]), SMALL_ALPHA),
            "gdn_dt": _small_vec(gdn_dt_bias[l], SMALL_ALPHA),
            "gdn_norm": gdn_norm[l].reshape(1, HEAD_DIM),
        }
        z = in_proj_call(h, w_in_z, l)

        y_a = lru_call(z, p, batch, seq)

        c = logf_cumsum_call(z, _small_vec(fox_f_bias[l], SMALL_F), batch, seq)
        c = c.reshape(batch, seq, LANE)[:, :, SMALL_F:SMALL_F + N_HEADS]
        c = jnp.transpose(c, (0, 2, 1))
        z3 = z.reshape(batch, seq, Z_COLS)
        y_b = fox_call(z3, c, norm_b[l].reshape(1, gw), batch, seq).reshape(m, gw)

        y_c = gdn_call(z, p, batch, seq)

        y_d = dilated_call(z3, norm_d[l].reshape(1, gw), batch, seq).reshape(m, gw)

        x1, hf = out_proj_call((y_a, y_b, y_c, y_d), w_out_b, l, xs,
                               norm_ffn[l].reshape(1, d))
        g = ffn_up_call(hf, w_up_b, l, ffn_conv_w[l], ffn_conv_b[l].reshape(1, 2 * D_FF), seq)
        last = l == depth - 1
        gain_next = norm_final if last else norm_mix[l + 1]
        xs, h = ffn_down_call(g, w_down_b, l, x1, gain_next.reshape(1, d),
                              F32 if last else BF16)
    return h.reshape(batch, seq, d)
```

```python
import functools
import math

import jax
import jax.numpy as jnp
from jax import lax
from jax.experimental import pallas as pl
from jax.experimental.pallas import tpu as pltpu

F32 = jnp.float32
BF16 = jnp.bfloat16

D_MODEL = 2048
GROUP_WIDTH = 512
HEAD_DIM = 128
N_HEADS = 4
LRU_BLOCKS = 8
LRU_BLOCK_DIM = 64
LRU_C = 8.0
SHORT_CONV = 4
FFN_CONV = 3
D_FF = 5632
GDN_CHUNK = 64
SPAN = 128
DILATIONS = (1, 4, 16)
EPS = 1e-6
NEG_INF = -1e30
ATTN_SCALE = HEAD_DIM ** -0.5
LOG2E = math.log2(math.e)

LANE = 128
SUBLANE = 8
VMEM_LIMIT = 52 * 1024 * 1024

COL_CQKV = 0
COL_AX = 1536
COL_AG = 2048
COL_CZ = 2560
COL_BQKV = 3072
COL_DQKV = 4608
COL_SMALL = 6144
Z_COLS = 6400
SMALL_F, SMALL_BETA, SMALL_ALPHA = 0, 4, 8


def _params(*sem):
    return pltpu.CompilerParams(dimension_semantics=sem, vmem_limit_bytes=VMEM_LIMIT)


def _sigmoid(x):
    return 1.0 / (1.0 + jnp.exp(-x))


def _softplus(x):
    return jnp.maximum(x, 0.0) + jnp.log1p(jnp.exp(-jnp.abs(x)))


def _silu(x):
    return x * _sigmoid(x)


def _gelu_tanh(x):
    c = math.sqrt(2.0 / math.pi)
    return 0.5 * x * (1.0 + jnp.tanh(c * (x + 0.044715 * (x * x * x))))


def _dot(a, b, precision=None):
    return jnp.dot(a, b, preferred_element_type=F32, precision=precision)


def _dot_nt(a, b):
    return lax.dot_general(a, b, (((1,), (1,)), ((), ())), preferred_element_type=F32)


def _dot_tn(a, b):
    return lax.dot_general(a, b, (((0,), (0,)), ((), ())), preferred_element_type=F32)


def _causal_conv(x, halo, w, taps):
    n = x.shape[0]
    cat = jnp.concatenate([halo, x], axis=0)
    y = w[taps - 1:taps, :] * x
    for k in range(taps - 1):
        shifted = pltpu.roll(cat, taps - 1 - k, 0)[SUBLANE:SUBLANE + n, :]
        y = y + w[k:k + 1, :] * shifted
    return y


def _scan_affine(a, u, h0):
    n, c = a.shape
    groups = n // SUBLANE
    a = a.reshape(groups, SUBLANE, c)
    u = u.reshape(groups, SUBLANE, c)
    pos = lax.broadcasted_iota(jnp.int32, a.shape, 1)
    s = 1
    while s < SUBLANE:
        keep = pos >= s
        a_s = jnp.where(keep, pltpu.roll(a, s, 1), 1.0)
        u_s = jnp.where(keep, pltpu.roll(u, s, 1), 0.0)
        u = a * u_s + u
        a = a * a_s
        s *= 2
    out = []
    carry = h0
    for g in range(groups):
        hg = u[g] + a[g] * carry
        carry = hg[SUBLANE - 1:SUBLANE, :]
        out.append(hg)
    return jnp.concatenate(out, axis=0)


def _scan_add(x, seg):
    row = lax.broadcasted_iota(jnp.int32, x.shape, 0)
    pos = row % seg
    s = 1
    while s < seg:
        x = x + jnp.where(pos >= s, pltpu.roll(x, s, 0), 0.0)
        s *= 2
    return x


def _rmsnorm_rows(x, gain):
    ms = jnp.mean(x * x, axis=-1, keepdims=True)
    return x * lax.rsqrt(ms + EPS) * gain


def _rmsnorm_kernel(x_ref, g_ref, o_ref):
    o_ref[...] = _rmsnorm_rows(x_ref[...], g_ref[...]).astype(o_ref.dtype)


def rmsnorm_call(x, gain, out_dtype, tm=512):
    m, d = x.shape
    return pl.pallas_call(
        _rmsnorm_kernel,
        grid=(m // tm,),
        in_specs=[pl.BlockSpec((tm, d), lambda i: (i, 0)),
                  pl.BlockSpec((1, d), lambda i: (0, 0))],
        out_specs=pl.BlockSpec((tm, d), lambda i: (i, 0)),
        out_shape=jax.ShapeDtypeStruct((m, d), out_dtype),
        compiler_params=_params("parallel"),
        name="rmsnorm",
    )(x, gain.reshape(1, d))


def _matmul_kernel(a_ref, b_ref, o_ref):
    o_ref[...] = _dot(a_ref[...], b_ref[...])


def in_proj_call(h, w, layer, tm=1024, tn=1280):
    m, k = h.shape
    n = w.shape[2]
    tm = min(tm, m)
    return pl.pallas_call(
        _matmul_kernel,
        grid=(m // tm, n // tn),
        in_specs=[pl.BlockSpec((tm, k), lambda i, j: (i, 0)),
                  pl.BlockSpec((None, k, tn), lambda i, j: (layer, 0, j))],
        out_specs=pl.BlockSpec((tm, tn), lambda i, j: (i, j)),
        out_shape=jax.ShapeDtypeStruct((m, n), F32),
        compiler_params=_params("parallel", "arbitrary"),
        name="in_proj",
    )(h, w)


def _group_mean_sq(h, gmat):
    h2 = h * h
    hi = h2.astype(BF16)
    lo = (h2 - hi.astype(F32)).astype(BF16)
    return _dot(hi, gmat) + _dot(lo, gmat)


def _lru_kernel(ax_ref, ag_ref, cw_ref, cb_ref, wa_ref, ba_ref, wx_ref, bx_ref, lam_ref,
                gmat_ref, gain_ref, o_ref, halo_ref, hprev_ref):
    tt = ax_ref.shape[0]

    @pl.when(pl.program_id(1) == 0)
    def _():
        halo_ref[...] = jnp.zeros_like(halo_ref)
        hprev_ref[...] = jnp.zeros_like(hprev_ref)

    x = ax_ref[...]
    xc = _causal_conv(x, halo_ref[...], cw_ref[...], SHORT_CONV) + cb_ref[...]
    halo_ref[...] = x[tt - SUBLANE:, :]
    xb = xc.astype(BF16)
    r = _sigmoid(_dot(xb, wa_ref[...]) + ba_ref[...])
    i = _sigmoid(_dot(xb, wx_ref[...]) + bx_ref[...])
    log_a = (-LRU_C) * r * _softplus(-lam_ref[...])
    a = jnp.exp(log_a)
    one_m_a2 = -jnp.tanh(log_a) * (a * a + 1.0)
    root = jnp.where(one_m_a2 > 0.0, one_m_a2 * lax.rsqrt(one_m_a2), 0.0)
    u = root * (i * xc)
    h = _scan_affine(a, u, hprev_ref[...])
    hprev_ref[...] = h[tt - 1:tt, :]
    y = h * lax.rsqrt(_group_mean_sq(h, gmat_ref[...]) + EPS) * gain_ref[...]
    o_ref[...] = (y * _gelu_tanh(ag_ref[...])).astype(o_ref.dtype)


def lru_call(z, p, batch, seq, tt=512):
    tt = min(tt, seq)
    nt = seq // tt
    w = GROUP_WIDTH
    row = lambda b, t: (b * nt + t)
    vec = pl.BlockSpec((1, w), lambda b, t: (0, 0))
    mat = pl.BlockSpec((w, w), lambda b, t: (0, 0))
    return pl.pallas_call(
        _lru_kernel,
        grid=(batch, nt),
        in_specs=[pl.BlockSpec((tt, w), lambda b, t: (row(b, t), COL_AX // w)),
                  pl.BlockSpec((tt, w), lambda b, t: (row(b, t), COL_AG // w)),
                  pl.BlockSpec((SHORT_CONV, w), lambda b, t: (0, 0)),
                  vec, mat, vec, mat, vec, vec, mat, vec],
        out_specs=pl.BlockSpec((tt, w), lambda b, t: (row(b, t), 0)),
        out_shape=jax.ShapeDtypeStruct((batch * seq, w), BF16),
        scratch_shapes=[pltpu.VMEM((SUBLANE, w), F32), pltpu.VMEM((1, w), F32)],
        compiler_params=_params("parallel", "arbitrary"),
        name="rg_lru",
    )(z, z, p["lru_conv_w"], p["lru_conv_b"], p["lru_wa"], p["lru_ba"], p["lru_wx"],
      p["lru_bx"], p["lru_lambda"], p["gmat64"], p["norm_a"])


def _logf_cumsum_kernel(s_ref, bias_ref, o_ref, carry_ref):
    tt = s_ref.shape[0]

    @pl.when(pl.program_id(1) == 0)
    def _():
        carry_ref[...] = jnp.zeros_like(carry_ref)

    log_f = -_softplus(-(s_ref[...] + bias_ref[...]))
    c = _scan_add(log_f, tt) + carry_ref[...]
    carry_ref[...] = c[tt - 1:tt, :]
    o_ref[...] = c


def logf_cumsum_call(z, bias, batch, seq, tt=1024):
    tt = min(tt, seq)
    nt = seq // tt
    return pl.pallas_call(
        _logf_cumsum_kernel,
        grid=(batch, nt),
        in_specs=[pl.BlockSpec((tt, LANE), lambda b, t: (b * nt + t, COL_SMALL // LANE)),
                  pl.BlockSpec((1, LANE), lambda b, t: (0, 0))],
        out_specs=pl.BlockSpec((tt, LANE), lambda b, t: (b * nt + t, 0)),
        out_shape=jax.ShapeDtypeStruct((batch * seq, LANE), F32),
        scratch_shapes=[pltpu.VMEM((1, LANE), F32)],
        compiler_params=_params("parallel", "arbitrary"),
        name="logf_cumsum",
    )(z, bias)


FOX_ROW_CHUNK = 64


def _fox_kernel(q_ref, k_ref, v_ref, cq_ref, ck_ref, gain_ref, o_ref, kb_ref, vb_ref,
                s_s, p_s, m_s, alpha_s, cq_s, acc_s, *, tq, tk):
    qi = pl.program_id(2)

    @pl.when(qi == 0)
    def _():
        kb_ref[...] = k_ref[...].astype(BF16)
        vb_ref[:, :HEAD_DIM] = v_ref[...].astype(BF16)
        vb_ref[:, HEAD_DIM:] = jnp.ones((vb_ref.shape[0], HEAD_DIM), BF16)

    q = (q_ref[...] * (ATTN_SCALE * LOG2E)).astype(BF16)
    cq_s[...] = jnp.broadcast_to(cq_ref[...] * LOG2E, (tq, LANE))
    m_s[...] = jnp.full((tq, LANE), NEG_INF, F32)
    acc_s[...] = jnp.zeros((tq, 2 * HEAD_DIM), F32)
    rc = min(FOX_ROW_CHUNK, tq)

    def lanes(x, n):
        return jnp.concatenate([x] * (n // LANE), axis=1)

    def scores(kv, slot):
        start = pl.multiple_of(kv * tk, tk)
        s_s[slot] = _dot_nt(q, kb_ref[pl.ds(start, tk), :]) - ck_ref[kv] * LOG2E

    def update(kv, slot, diagonal):
        for r0 in range(0, tq, rc):
            rows = slice(r0, r0 + rc)
            s = s_s[slot, rows, :]
            if diagonal:
                row = lax.broadcasted_iota(jnp.int32, s.shape, 0) + r0
                col = lax.broadcasted_iota(jnp.int32, s.shape, 1)
                s = jnp.where(col <= row, s, NEG_INF)
            m_old = m_s[rows, :]
            m_new = jnp.maximum(m_old, jnp.max(s, axis=-1, keepdims=True) + cq_s[rows, :])
            p_s[rows, :] = jnp.exp2(s - lanes(m_new - cq_s[rows, :], tk)).astype(BF16)
            alpha_s[rows, :] = jnp.exp2(m_old - m_new)
            m_s[rows, :] = m_new
        start = pl.multiple_of(kv * tk, tk)
        pv = _dot(p_s[...], vb_ref[pl.ds(start, tk), :])
        acc_s[...] = lanes(alpha_s[...], 2 * HEAD_DIM) * acc_s[...] + pv

    def pair(jj, carry):
        scores(2 * jj + 1, 1)
        update(2 * jj, 0, False)
        scores(2 * jj + 2, 0)
        update(2 * jj + 1, 1, False)
        return carry

    scores(0, 0)
    lax.fori_loop(0, qi // 2, pair, 0)

    @pl.when(qi % 2 == 1)
    def _():
        scores(qi, 1)
        update(qi - 1, 0, False)
        update(qi, 1, True)

    @pl.when(qi % 2 == 0)
    def _():
        update(qi, 0, True)

    acc = acc_s[...]
    o = acc[:, :HEAD_DIM] / acc[:, HEAD_DIM:]
    o = o * lax.rsqrt(jnp.mean(o * o, axis=-1, keepdims=True) + EPS) * gain_ref[...]
    o_ref[...] = o.astype(o_ref.dtype)


def fox_call(z3, c, gain, batch, seq, tq=512):
    tq = min(tq, seq)
    tk = tq
    nq = seq // tq
    nk = seq // tk
    qb, kb, vb = (COL_BQKV // LANE, (COL_BQKV + GROUP_WIDTH) // LANE,
                  (COL_BQKV + 2 * GROUP_WIDTH) // LANE)
    c_col = c.reshape(batch, N_HEADS, seq, 1)
    c_row = c.reshape(batch, N_HEADS, nk, 1, tk)
    return pl.pallas_call(
        functools.partial(_fox_kernel, tq=tq, tk=tk),
        grid=(batch, N_HEADS, nq),
        in_specs=[pl.BlockSpec((None, tq, LANE), lambda b, h, i: (b, i, qb + h)),
                  pl.BlockSpec((None, seq, LANE), lambda b, h, i: (b, 0, kb + h)),
                  pl.BlockSpec((None, seq, LANE), lambda b, h, i: (b, 0, vb + h)),
                  pl.BlockSpec((None, None, tq, 1), lambda b, h, i: (b, h, i, 0)),
                  pl.BlockSpec((None, None, nk, 1, tk), lambda b, h, i: (b, h, 0, 0, 0)),
                  pl.BlockSpec((1, LANE), lambda b, h, i: (0, h))],
        out_specs=pl.BlockSpec((None, tq, LANE), lambda b, h, i: (b, i, h)),
        out_shape=jax.ShapeDtypeStruct((batch, seq, GROUP_WIDTH), BF16),
        scratch_shapes=[pltpu.VMEM((seq, LANE), BF16), pltpu.VMEM((seq, 2 * LANE), BF16),
                        pltpu.VMEM((2, tq, tk), F32), pltpu.VMEM((tq, tk), BF16),
                        pltpu.VMEM((tq, LANE), F32), pltpu.VMEM((tq, LANE), F32),
                        pltpu.VMEM((tq, LANE), F32), pltpu.VMEM((tq, 2 * HEAD_DIM), F32)],
        compiler_params=_params("parallel", "parallel", "arbitrary"),
        name="fox_attention",
    )(z3, z3, z3, c_col, c_row, gain)


def _l2norm(t):
    return t * lax.rsqrt(jnp.sum(t * t, axis=-1, keepdims=True) + EPS)


def _bmm(a, b):
    return lax.dot_general(a, b, (((2,), (1,)), ((0,), (0,))), preferred_element_type=F32)


def _bmm_nt(a, b):
    return lax.dot_general(a, b, (((2,), (2,)), ((0,), (0,))), preferred_element_type=F32)


def _bmm_tn(a, b):
    return lax.dot_general(a, b, (((1,), (1,)), ((0,), (0,))), preferred_element_type=F32)


def _neumann_tail(p):
    c = p.shape[-1]
    a = p
    q = p
    cover = 2
    while cover < c:
        qb = q.astype(BF16)
        q = _bmm(qb, qb)
        a = a + q + _bmm(a.astype(BF16), q.astype(BF16))
        cover *= 2
    return a


def _gdn_kernel(x_ref, zg_ref, s_ref, cw_ref, aneg_ref, dt_ref, gain_ref, o_ref,
                halo_ref, qkv_ref, state_ref):
    tt = x_ref.shape[0]
    c = GDN_CHUNK
    nc = tt // c
    w = GROUP_WIDTH
    hd = HEAD_DIM

    @pl.when(pl.program_id(1) == 0)
    def _():
        halo_ref[...] = jnp.zeros_like(halo_ref)
        state_ref[...] = jnp.zeros_like(state_ref)

    x = x_ref[...]
    qkv_ref[...] = _silu(_causal_conv(x, halo_ref[...], cw_ref[...], SHORT_CONV))
    halo_ref[...] = x[tt - SUBLANE:, :]
    small = s_ref[...]
    beta_all = _sigmoid(small)
    gc_all = _scan_add(aneg_ref[...] * _softplus(small + dt_ref[...]), c)

    pairs = [(ci, h) for ci in range(nc) for h in range(N_HEADS)]

    def gather(col0):
        return jnp.stack([qkv_ref[ci * c:(ci + 1) * c, col0 + h * hd:col0 + (h + 1) * hd]
                          for ci, h in pairs])

    q = _l2norm(gather(0)) * ATTN_SCALE
    k = _l2norm(gather(w))
    v = gather(2 * w)
    beta = jnp.stack([beta_all[ci * c:(ci + 1) * c, SMALL_BETA + h:SMALL_BETA + h + 1]
                      for ci, h in pairs])
    gc = jnp.stack([gc_all[ci * c:(ci + 1) * c, SMALL_ALPHA + h:SMALL_ALPHA + h + 1]
                    for ci, h in pairs])
    gc_t = [gc_all[ci * c:(ci + 1) * c, :].T for ci in range(nc)]
    gc_row = jnp.stack([gc_t[ci][SMALL_ALPHA + h:SMALL_ALPHA + h + 1, :]
                        for ci, h in pairs])

    row = lax.broadcasted_iota(jnp.int32, (c, c), 0)
    col = lax.broadcasted_iota(jnp.int32, (c, c), 1)
    tril = col <= row
    strict = col < row
    decay = jnp.where(tril, jnp.exp(jnp.where(tril, gc - gc_row, 0.0)), 0.0)
    kb = k.astype(BF16)
    kbeta = k * beta
    kk = _bmm_nt(kbeta.astype(BF16), kb) * decay
    tail = _neumann_tail(jnp.where(strict, -kk, 0.0))
    rhs = jnp.concatenate([v * beta, kbeta * jnp.exp(gc)], axis=-1)
    sol = rhs + _bmm(tail.astype(BF16), rhs.astype(BF16))
    u = sol[:, :, :hd]
    wy = sol[:, :, hd:]
    qk = jnp.where(tril, _bmm_nt(q.astype(BF16), kb) * decay, 0.0).astype(BF16)
    g_last = gc[:, c - 1:c, :]
    k_dec = (k * jnp.exp(g_last - gc)).astype(BF16)
    e_last = jnp.exp(g_last)
    wq = jnp.concatenate([wy.astype(BF16), (q * jnp.exp(gc)).astype(BF16)], axis=1)

    state = state_ref[...]
    for ci in range(nc):
        sl = slice(ci * N_HEADS, (ci + 1) * N_HEADS)
        ws = _bmm(wq[sl], state.astype(BF16))
        v_new = (u[sl] - ws[:, :c, :]).astype(BF16)
        o = ws[:, c:, :] + _bmm(qk[sl], v_new)
        state = state * e_last[sl] + _bmm_tn(k_dec[sl], v_new)
        o = o * lax.rsqrt(jnp.mean(o * o, axis=-1, keepdims=True) + EPS) * gain_ref[...]
        for h in range(N_HEADS):
            zg = zg_ref[ci * c:(ci + 1) * c, h * hd:(h + 1) * hd]
            o_ref[ci * c:(ci + 1) * c, h * hd:(h + 1) * hd] = (o[h] * _silu(zg)).astype(o_ref.dtype)
    state_ref[...] = state


def gdn_call(z, p, batch, seq, tt=512):
    tt = min(tt, seq)
    nt = seq // tt
    w = GROUP_WIDTH
    vec = pl.BlockSpec((1, LANE), lambda b, t: (0, 0))
    return pl.pallas_call(
        _gdn_kernel,
        grid=(batch, nt),
        in_specs=[pl.BlockSpec((tt, 3 * w), lambda b, t: (b * nt + t, COL_CQKV // (3 * w))),
                  pl.BlockSpec((tt, w), lambda b, t: (b * nt + t, COL_CZ // w)),
                  pl.BlockSpec((tt, LANE), lambda b, t: (b * nt + t, COL_SMALL // LANE)),
                  pl.BlockSpec((SHORT_CONV, 3 * w), lambda b, t: (0, 0)),
                  vec, vec, vec],
        out_specs=pl.BlockSpec((tt, w), lambda b, t: (b * nt + t, 0)),
        out_shape=jax.ShapeDtypeStruct((batch * seq, w), BF16),
        scratch_shapes=[pltpu.VMEM((SUBLANE, 3 * w), F32),
                        pltpu.VMEM((tt, 3 * w), F32),
                        pltpu.VMEM((N_HEADS, HEAD_DIM, HEAD_DIM), F32)],
        compiler_params=_params("parallel", "arbitrary"),
        name="gated_deltanet",
    )(z, z, z, p["gdn_conv_w"], p["gdn_aneg"], p["gdn_dt"], p["gdn_norm"])


def _dilated_kernel(q_ref, kc_ref, kp_ref, vc_ref, vp_ref, gain_ref, o_ref,
                    m_ref, acc_ref, l_ref):
    tt = q_ref.shape[0]
    first = pl.program_id(2) == 0
    i = lax.broadcasted_iota(jnp.int32, (SPAN, 2 * SPAN), 0)
    j = lax.broadcasted_iota(jnp.int32, (SPAN, 2 * SPAN), 1)
    band = jnp.logical_and(j >= i, j <= i + SPAN)
    band_first = jnp.logical_and(band, jnp.logical_or(j >= SPAN, jnp.logical_not(first)))
    ones = jnp.ones((2 * SPAN, HEAD_DIM), BF16)

    def rows(ref, start, n, dil):
        return ref[pl.ds(start, n, stride=dil), :] if dil > 1 else ref[pl.ds(start, n), :]

    for g, dil in enumerate(DILATIONS):
        for r in range(dil):
            for sb in range(tt // (dil * SPAN)):
                q0 = r + dil * SPAN * sb
                q = (rows(q_ref, q0, SPAN, dil) * ATTN_SCALE).astype(BF16)
                if sb == 0:
                    p0 = tt - dil * SPAN + r
                    k = jnp.concatenate([rows(kp_ref, p0, SPAN, dil), rows(kc_ref, r, SPAN, dil)], 0)
                    v = jnp.concatenate([rows(vp_ref, p0, SPAN, dil), rows(vc_ref, r, SPAN, dil)], 0)
                else:
                    k = rows(kc_ref, q0 - dil * SPAN, 2 * SPAN, dil)
                    v = rows(vc_ref, q0 - dil * SPAN, 2 * SPAN, dil)
                s = _dot_nt(q, k.astype(BF16))
                s = jnp.where(band_first if sb == 0 else band, s, NEG_INF)
                v1 = jnp.concatenate([v.astype(BF16), ones], axis=1)
                m_blk = jnp.broadcast_to(jnp.max(s, axis=-1, keepdims=True), (SPAN, LANE))
                if g == 0:
                    m_new = m_blk
                else:
                    m_old = rows(m_ref, q0, SPAN, dil)
                    m_new = jnp.maximum(m_old, m_blk)
                p = jnp.exp(s - jnp.concatenate([m_new, m_new], axis=1))
                pv = _dot(p.astype(BF16), v1)
                acc, l = pv[:, :HEAD_DIM], pv[:, HEAD_DIM:]
                if g > 0:
                    a = jnp.exp(m_old - m_new)
                    acc = acc + a * rows(acc_ref, q0, SPAN, dil)
                    l = l + a * rows(l_ref, q0, SPAN, dil)
                dst = pl.ds(q0, SPAN, stride=dil) if dil > 1 else pl.ds(q0, SPAN)
                m_ref[dst, :] = m_new
                acc_ref[dst, :] = acc
                l_ref[dst, :] = l
    y = acc_ref[...] / l_ref[...]
    y = y * lax.rsqrt(jnp.mean(y * y, axis=-1, keepdims=True) + EPS) * gain_ref[...]
    o_ref[...] = y.astype(o_ref.dtype)


def dilated_call(z3, gain, batch, seq):
    tt = max(DILATIONS) * SPAN
    assert seq % tt == 0
    nt = seq // tt
    qb, kb, vb = (COL_DQKV // LANE, (COL_DQKV + GROUP_WIDTH) // LANE,
                  (COL_DQKV + 2 * GROUP_WIDTH) // LANE)

    def cur(off):
        return pl.BlockSpec((None, tt, LANE), lambda b, h, i: (b, i, off + h))

    def prev(off):
        return pl.BlockSpec((None, tt, LANE), lambda b, h, i: (b, jnp.maximum(i - 1, 0), off + h))

    return pl.pallas_call(
        _dilated_kernel,
        grid=(batch, N_HEADS, nt),
        in_specs=[cur(qb), cur(kb), prev(kb), cur(vb), prev(vb),
                  pl.BlockSpec((1, LANE), lambda b, h, i: (0, h))],
        out_specs=pl.BlockSpec((None, tt, LANE), lambda b, h, i: (b, i, h)),
        out_shape=jax.ShapeDtypeStruct((batch, seq, GROUP_WIDTH), BF16),
        scratch_shapes=[pltpu.VMEM((tt, LANE), F32), pltpu.VMEM((tt, HEAD_DIM), F32),
                        pltpu.VMEM((tt, LANE), F32)],
        compiler_params=_params("parallel", "parallel", "arbitrary"),
        name="dilated_attention",
    )(z3, z3, z3, z3, z3, gain)


def _out_proj_kernel(ya, yb, yc, yd, w_ref, x_ref, g_ref, x1_ref, h_ref):
    w = GROUP_WIDTH
    acc = x_ref[...]
    for n, y in enumerate((ya, yb, yc, yd)):
        acc = acc + _dot(y[...], w_ref[n * w:(n + 1) * w, :])
    x1_ref[...] = acc
    h_ref[...] = _rmsnorm_rows(acc, g_ref[...]).astype(h_ref.dtype)


def out_proj_call(ys, w_out, layer, x, gain, tm=512):
    m, d = x.shape
    tm = min(tm, m)
    yblk = pl.BlockSpec((tm, GROUP_WIDTH), lambda i: (i, 0))
    xblk = pl.BlockSpec((tm, d), lambda i: (i, 0))
    return pl.pallas_call(
        _out_proj_kernel,
        grid=(m // tm,),
        in_specs=[yblk] * 4 + [pl.BlockSpec((None, d, d), lambda i: (layer, 0, 0),
                                            pipeline_mode=pl.Buffered(1)), xblk,
                               pl.BlockSpec((1, d), lambda i: (0, 0))],
        out_specs=[xblk, xblk],
        out_shape=[jax.ShapeDtypeStruct((m, d), F32), jax.ShapeDtypeStruct((m, d), BF16)],
        compiler_params=_params("parallel"),
        name="out_proj",
    )(*ys, w_out, x, gain)


FFN_SUB_ROWS = 256


def _ffn_up_kernel(h_ref, wu32_ref, wg32_ref, cwu_ref, cwg_ref, cbu_ref, cbg_ref, o_ref,
                   tail_u_ref, tail_g_ref, wu_ref, wg_ref, *, tiles_per_seq):
    tm = h_ref.shape[0]

    @pl.when(pl.program_id(1) == 0)
    def _():
        wu_ref[...] = wu32_ref[...].astype(BF16)
        wg_ref[...] = wg32_ref[...].astype(BF16)

    @pl.when(pl.program_id(1) % tiles_per_seq == 0)
    def _():
        tail_u_ref[...] = jnp.zeros_like(tail_u_ref)
        tail_g_ref[...] = jnp.zeros_like(tail_g_ref)

    sub = min(FFN_SUB_ROWS, tm)
    tail_u = tail_u_ref[...]
    tail_g = tail_g_ref[...]
    for r0 in range(0, tm, sub):
        h = h_ref[r0:r0 + sub, :]
        u = _dot(h, wu_ref[...])
        g = _dot(h, wg_ref[...])
        up = _causal_conv(u, tail_u, cwu_ref[...], FFN_CONV) + cbu_ref[...]
        gate = _causal_conv(g, tail_g, cwg_ref[...], FFN_CONV) + cbg_ref[...]
        tail_u = u[sub - SUBLANE:, :]
        tail_g = g[sub - SUBLANE:, :]
        o_ref[r0:r0 + sub, :] = (_silu(gate) * up).astype(o_ref.dtype)
    tail_u_ref[...] = tail_u
    tail_g_ref[...] = tail_g


def ffn_up_call(h, w_up, layer, conv_w, conv_b, seq, tm=1024, tn=512):
    m, d = h.shape
    tm = min(tm, seq)
    nj = D_FF // tn
    return pl.pallas_call(
        functools.partial(_ffn_up_kernel, tiles_per_seq=seq // tm),
        grid=(nj, m // tm),
        in_specs=[pl.BlockSpec((tm, d), lambda j, i: (i, 0)),
                  pl.BlockSpec((None, d, tn), lambda j, i: (layer, 0, j)),
                  pl.BlockSpec((None, d, tn), lambda j, i: (layer, 0, j + nj)),
                  pl.BlockSpec((FFN_CONV, tn), lambda j, i: (0, j)),
                  pl.BlockSpec((FFN_CONV, tn), lambda j, i: (0, j + nj)),
                  pl.BlockSpec((1, tn), lambda j, i: (0, j)),
                  pl.BlockSpec((1, tn), lambda j, i: (0, j + nj))],
        out_specs=pl.BlockSpec((tm, tn), lambda j, i: (i, j)),
        out_shape=jax.ShapeDtypeStruct((m, D_FF), BF16),
        scratch_shapes=[pltpu.VMEM((SUBLANE, tn), F32), pltpu.VMEM((SUBLANE, tn), F32),
                        pltpu.VMEM((d, tn), BF16), pltpu.VMEM((d, tn), BF16)],
        compiler_params=_params("parallel", "arbitrary"),
        name="ffn_up",
    )(h, w_up, w_up, conv_w, conv_w, conv_b, conv_b)


def _ffn_down_kernel(g_ref, w_ref, x_ref, gain_ref, x2_ref, hn_ref):
    x2 = x_ref[...] + _dot(g_ref[...], w_ref[...])
    x2_ref[...] = x2
    hn_ref[...] = _rmsnorm_rows(x2, gain_ref[...]).astype(hn_ref.dtype)


def ffn_down_call(g, w_down, layer, x, gain, hn_dtype, tm=256):
    m, d = x.shape
    tm = min(tm, m)
    kdim = g.shape[1]
    xblk = pl.BlockSpec((tm, d), lambda i: (i, 0))
    return pl.pallas_call(
        _ffn_down_kernel,
        grid=(m // tm,),
        in_specs=[pl.BlockSpec((tm, kdim), lambda i: (i, 0)),
                  pl.BlockSpec((None, kdim, d), lambda i: (layer, 0, 0),
                               pipeline_mode=pl.Buffered(1)),
                  xblk, pl.BlockSpec((1, d), lambda i: (0, 0))],
        out_specs=[xblk, xblk],
        out_shape=[jax.ShapeDtypeStruct((m, d), F32), jax.ShapeDtypeStruct((m, d), hn_dtype)],
        compiler_params=_params("parallel"),
        name="ffn_down",
    )(g, w_down, x, gain)


def _block_diag(blocks):
    n, r, c = blocks.shape
    eye = jnp.eye(n, dtype=blocks.dtype)
    return (eye[:, None, :, None] * blocks[:, :, None, :]).reshape(n * r, n * c)


def _small_vec(values, offset):
    return jnp.zeros((1, LANE), F32).at[0, offset:offset + values.shape[0]].set(values)


def _w_in_segments():
    gw = GROUP_WIDTH
    sizes = (gw, gw, 3 * gw, N_HEADS, 3 * gw, gw, N_HEADS, N_HEADS, 3 * gw)
    dsts = (COL_AX, COL_AG, COL_BQKV, COL_SMALL + SMALL_F, COL_CQKV, COL_CZ,
            COL_SMALL + SMALL_BETA, COL_SMALL + SMALL_ALPHA, COL_DQKV)
    segs, src = [], 0
    for size, dst in zip(sizes, dsts):
        segs.append((src, dst, size))
        src += size
    return segs, src


def _prep_w_in_kernel(w_ref, o_ref):
    segs, _ = _w_in_segments()
    o_ref[:, COL_SMALL:] = jnp.zeros((o_ref.shape[0], Z_COLS - COL_SMALL), o_ref.dtype)
    for src, dst, size in segs:
        lo = (src // LANE) * LANE
        hi = min(-(-(src + size) // LANE) * LANE, w_ref.shape[1])
        tile = w_ref[:, lo:hi]
        o_ref[:, dst:dst + size] = tile[:, src - lo:src - lo + size].astype(o_ref.dtype)


def prep_w_in_call(w_in, tk=256):
    depth, k, n = w_in.shape
    assert n == _w_in_segments()[1]
    return pl.pallas_call(
        _prep_w_in_kernel,
        grid=(depth, k // tk),
        in_specs=[pl.BlockSpec((None, tk, n), lambda l, i: (l, i, 0))],
        out_specs=pl.BlockSpec((None, tk, Z_COLS), lambda l, i: (l, i, 0)),
        out_shape=jax.ShapeDtypeStruct((depth, k, Z_COLS), BF16),
        compiler_params=_params("parallel", "parallel"),
        name="prep_w_in",
    )(w_in)


def kernel(x, norm_mix, w_in, lru_conv_w, lru_conv_b, lru_wa, lru_ba, lru_wx, lru_bx,
           lru_lambda, fox_f_bias, gdn_conv_w, gdn_a_log, gdn_dt_bias, gdn_norm,
           norm_a, norm_b, norm_d, w_out, norm_ffn, ffn_w_up, ffn_conv_w, ffn_conv_b,
           ffn_w_down, norm_final):
    batch, seq, d = x.shape
    depth = w_in.shape[0]
    m = batch * seq
    gw = GROUP_WIDTH
    gmat64 = _block_diag(jnp.full((LRU_BLOCKS, LRU_BLOCK_DIM, LRU_BLOCK_DIM),
                                  1.0 / LRU_BLOCK_DIM, F32)).astype(BF16)
    xs = x.reshape(m, d)
    w_in_z = prep_w_in_call(w_in)
    w_out_b = w_out.astype(BF16)
    w_down_b = ffn_w_down.astype(BF16)
    h = rmsnorm_call(xs, norm_mix[0], BF16)
    for l in range(depth):
        p = {
            "lru_conv_w": lru_conv_w[l], "lru_conv_b": lru_conv_b[l].reshape(1, gw),
            "lru_wa": _block_diag(lru_wa[l]).astype(BF16), "lru_ba": lru_ba[l].reshape(1, gw),
            "lru_wx": _block_diag(lru_wx[l]).astype(BF16), "lru_bx": lru_bx[l].reshape(1, gw),
            "lru_lambda": lru_lambda[l].reshape(1, gw), "gmat64": gmat64,
            "norm_a": norm_a[l].reshape(1, gw),
            "gdn_conv_w": gdn_conv_w[l],
            "gdn_aneg": _small_vec(-jnp.exp(gdn_a_log[l]), SMALL_ALPHA),
            "gdn_dt": _small_vec(gdn_dt_bias[l], SMALL_ALPHA),
            "gdn_norm": gdn_norm[l].reshape(1, HEAD_DIM),
        }
        z = in_proj_call(h, w_in_z, l)

        y_a = lru_call(z, p, batch, seq)

        c = logf_cumsum_call(z, _small_vec(fox_f_bias[l], SMALL_F), batch, seq)
        c = c.reshape(batch, seq, LANE)[:, :, SMALL_F:SMALL_F + N_HEADS]
        c = jnp.transpose(c, (0, 2, 1))
        z3 = z.reshape(batch, seq, Z_COLS)
        y_b = fox_call(z3, c, norm_b[l].reshape(1, gw), batch, seq).reshape(m, gw)

        y_c = gdn_call(z, p, batch, seq)

        y_d = dilated_call(z3, norm_d[l].reshape(1, gw), batch, seq).reshape(m, gw)

        x1, hf = out_proj_call((y_a, y_b, y_c, y_d), w_out_b, l, xs,
                               norm_ffn[l].reshape(1, d))
        g = ffn_up_call(hf, ffn_w_up, l, ffn_conv_w[l], ffn_conv_b[l].reshape(1, 2 * D_FF), seq)
        last = l == depth - 1
        gain_next = norm_final if last else norm_mix[l + 1]
        xs, h = ffn_down_call(g, w_down_b, l, x1, gain_next.reshape(1, d),
                              F32 if last else BF16)
    return h.reshape(batch, seq, d)
```

```python
import functools
import math

import jax
import jax.numpy as jnp
from jax import lax
from jax.experimental import pallas as pl
from jax.experimental.pallas import tpu as pltpu

F32 = jnp.float32
BF16 = jnp.bfloat16

D_MODEL = 2048
GROUP_WIDTH = 512
HEAD_DIM = 128
N_HEADS = 4
LRU_BLOCKS = 8
LRU_BLOCK_DIM = 64
LRU_C = 8.0
SHORT_CONV = 4
FFN_CONV = 3
D_FF = 5632
GDN_CHUNK = 64
SPAN = 128
DILATIONS = (1, 4, 16)
EPS = 1e-6
NEG_INF = -1e30
ATTN_SCALE = HEAD_DIM ** -0.5
LOG2E = math.log2(math.e)

LANE = 128
SUBLANE = 8
VMEM_LIMIT = 52 * 1024 * 1024

COL_CQKV = 0
COL_AX = 1536
COL_AG = 2048
COL_CZ = 2560
COL_BQKV = 3072
COL_DQKV = 4608
COL_SMALL = 6144
Z_COLS = 6400
SMALL_F, SMALL_BETA, SMALL_ALPHA = 0, 4, 8


def _params(*sem):
    return pltpu.CompilerParams(dimension_semantics=sem, vmem_limit_bytes=VMEM_LIMIT)


def _sigmoid(x):
    return 1.0 / (1.0 + jnp.exp(-x))


def _softplus(x):
    return jnp.maximum(x, 0.0) + jnp.log1p(jnp.exp(-jnp.abs(x)))


def _silu(x):
    return x * _sigmoid(x)


def _gelu_tanh(x):
    c = math.sqrt(2.0 / math.pi)
    return 0.5 * x * (1.0 + jnp.tanh(c * (x + 0.044715 * (x * x * x))))


def _dot(a, b, precision=None):
    return jnp.dot(a, b, preferred_element_type=F32, precision=precision)


def _dot_nt(a, b):
    return lax.dot_general(a, b, (((1,), (1,)), ((), ())), preferred_element_type=F32)


def _dot_tn(a, b):
    return lax.dot_general(a, b, (((0,), (0,)), ((), ())), preferred_element_type=F32)


def _causal_conv(x, halo, w, taps):
    n = x.shape[0]
    cat = jnp.concatenate([halo, x], axis=0)
    y = w[taps - 1:taps, :] * x
    for k in range(taps - 1):
        shifted = pltpu.roll(cat, taps - 1 - k, 0)[SUBLANE:SUBLANE + n, :]
        y = y + w[k:k + 1, :] * shifted
    return y


def _scan_affine(a, u, h0):
    n, c = a.shape
    groups = n // SUBLANE
    a = a.reshape(groups, SUBLANE, c)
    u = u.reshape(groups, SUBLANE, c)
    pos = lax.broadcasted_iota(jnp.int32, a.shape, 1)
    s = 1
    while s < SUBLANE:
        keep = pos >= s
        a_s = jnp.where(keep, pltpu.roll(a, s, 1), 1.0)
        u_s = jnp.where(keep, pltpu.roll(u, s, 1), 0.0)
        u = a * u_s + u
        a = a * a_s
        s *= 2
    out = []
    carry = h0
    for g in range(groups):
        hg = u[g] + a[g] * carry
        carry = hg[SUBLANE - 1:SUBLANE, :]
        out.append(hg)
    return jnp.concatenate(out, axis=0)


def _scan_add(x, seg):
    row = lax.broadcasted_iota(jnp.int32, x.shape, 0)
    pos = row % seg
    s = 1
    while s < seg:
        x = x + jnp.where(pos >= s, pltpu.roll(x, s, 0), 0.0)
        s *= 2
    return x


def _rmsnorm_rows(x, gain):
    ms = jnp.mean(x * x, axis=-1, keepdims=True)
    return x * lax.rsqrt(ms + EPS) * gain


def _rmsnorm_kernel(x_ref, g_ref, o_ref):
    o_ref[...] = _rmsnorm_rows(x_ref[...], g_ref[...]).astype(o_ref.dtype)


def rmsnorm_call(x, gain, out_dtype, tm=512):
    m, d = x.shape
    return pl.pallas_call(
        _rmsnorm_kernel,
        grid=(m // tm,),
        in_specs=[pl.BlockSpec((tm, d), lambda i: (i, 0)),
                  pl.BlockSpec((1, d), lambda i: (0, 0))],
        out_specs=pl.BlockSpec((tm, d), lambda i: (i, 0)),
        out_shape=jax.ShapeDtypeStruct((m, d), out_dtype),
        compiler_params=_params("parallel"),
        name="rmsnorm",
    )(x, gain.reshape(1, d))


def _matmul_kernel(a_ref, b_ref, o_ref):
    o_ref[...] = _dot(a_ref[...], b_ref[...])


def in_proj_call(h, w, layer, tm=1024, tn=1280):
    m, k = h.shape
    n = w.shape[2]
    tm = min(tm, m)
    return pl.pallas_call(
        _matmul_kernel,
        grid=(m // tm, n // tn),
        in_specs=[pl.BlockSpec((tm, k), lambda i, j: (i, 0)),
                  pl.BlockSpec((None, k, tn), lambda i, j: (layer, 0, j))],
        out_specs=pl.BlockSpec((tm, tn), lambda i, j: (i, j)),
        out_shape=jax.ShapeDtypeStruct((m, n), F32),
        compiler_params=_params("parallel", "arbitrary"),
        name="in_proj",
    )(h, w)


def _group_mean_sq(h, gmat):
    h2 = h * h
    hi = h2.astype(BF16)
    lo = (h2 - hi.astype(F32)).astype(BF16)
    return _dot(hi, gmat) + _dot(lo, gmat)


def _lru_kernel(ax_ref, ag_ref, cw_ref, cb_ref, wa_ref, ba_ref, wx_ref, bx_ref, lam_ref,
                gmat_ref, gain_ref, o_ref, halo_ref, hprev_ref):
    tt = ax_ref.shape[0]

    @pl.when(pl.program_id(1) == 0)
    def _():
        halo_ref[...] = jnp.zeros_like(halo_ref)
        hprev_ref[...] = jnp.zeros_like(hprev_ref)

    x = ax_ref[...]
    xc = _causal_conv(x, halo_ref[...], cw_ref[...], SHORT_CONV) + cb_ref[...]
    halo_ref[...] = x[tt - SUBLANE:, :]
    xb = xc.astype(BF16)
    r = _sigmoid(_dot(xb, wa_ref[...]) + ba_ref[...])
    i = _sigmoid(_dot(xb, wx_ref[...]) + bx_ref[...])
    log_a = (-LRU_C) * r * _softplus(-lam_ref[...])
    a = jnp.exp(log_a)
    one_m_a2 = -jnp.tanh(log_a) * (a * a + 1.0)
    root = jnp.where(one_m_a2 > 0.0, one_m_a2 * lax.rsqrt(one_m_a2), 0.0)
    u = root * (i * xc)
    h = _scan_affine(a, u, hprev_ref[...])
    hprev_ref[...] = h[tt - 1:tt, :]
    y = h * lax.rsqrt(_group_mean_sq(h, gmat_ref[...]) + EPS) * gain_ref[...]
    o_ref[...] = (y * _gelu_tanh(ag_ref[...])).astype(o_ref.dtype)


def lru_call(z, p, batch, seq, tt=512):
    tt = min(tt, seq)
    nt = seq // tt
    w = GROUP_WIDTH
    row = lambda b, t: (b * nt + t)
    vec = pl.BlockSpec((1, w), lambda b, t: (0, 0))
    mat = pl.BlockSpec((w, w), lambda b, t: (0, 0))
    return pl.pallas_call(
        _lru_kernel,
        grid=(batch, nt),
        in_specs=[pl.BlockSpec((tt, w), lambda b, t: (row(b, t), COL_AX // w)),
                  pl.BlockSpec((tt, w), lambda b, t: (row(b, t), COL_AG // w)),
                  pl.BlockSpec((SHORT_CONV, w), lambda b, t: (0, 0)),
                  vec, mat, vec, mat, vec, vec, mat, vec],
        out_specs=pl.BlockSpec((tt, w), lambda b, t: (row(b, t), 0)),
        out_shape=jax.ShapeDtypeStruct((batch * seq, w), BF16),
        scratch_shapes=[pltpu.VMEM((SUBLANE, w), F32), pltpu.VMEM((1, w), F32)],
        compiler_params=_params("parallel", "arbitrary"),
        name="rg_lru",
    )(z, z, p["lru_conv_w"], p["lru_conv_b"], p["lru_wa"], p["lru_ba"], p["lru_wx"],
      p["lru_bx"], p["lru_lambda"], p["gmat64"], p["norm_a"])


def _logf_cumsum_kernel(s_ref, bias_ref, o_ref, carry_ref):
    tt = s_ref.shape[0]

    @pl.when(pl.program_id(1) == 0)
    def _():
        carry_ref[...] = jnp.zeros_like(carry_ref)

    log_f = -_softplus(-(s_ref[...] + bias_ref[...]))
    c = _scan_add(log_f, tt) + carry_ref[...]
    carry_ref[...] = c[tt - 1:tt, :]
    o_ref[...] = c


def logf_cumsum_call(z, bias, batch, seq, tt=1024):
    tt = min(tt, seq)
    nt = seq // tt
    return pl.pallas_call(
        _logf_cumsum_kernel,
        grid=(batch, nt),
        in_specs=[pl.BlockSpec((tt, LANE), lambda b, t: (b * nt + t, COL_SMALL // LANE)),
                  pl.BlockSpec((1, LANE), lambda b, t: (0, 0))],
        out_specs=pl.BlockSpec((tt, LANE), lambda b, t: (b * nt + t, 0)),
        out_shape=jax.ShapeDtypeStruct((batch * seq, LANE), F32),
        scratch_shapes=[pltpu.VMEM((1, LANE), F32)],
        compiler_params=_params("parallel", "arbitrary"),
        name="logf_cumsum",
    )(z, bias)


FOX_ROW_CHUNK = 64


def _fox_kernel(q_ref, k_ref, v_ref, cq_ref, ck_ref, gain_ref, o_ref, kb_ref, vb_ref,
                s_s, p_s, m_s, alpha_s, cq_s, acc_s, *, tq, tk):
    qi = pl.program_id(2)

    @pl.when(qi == 0)
    def _():
        kb_ref[...] = k_ref[...].astype(BF16)
        vb_ref[:, :HEAD_DIM] = v_ref[...].astype(BF16)
        vb_ref[:, HEAD_DIM:] = jnp.ones((vb_ref.shape[0], HEAD_DIM), BF16)

    q = (q_ref[...] * (ATTN_SCALE * LOG2E)).astype(BF16)
    cq_s[...] = jnp.broadcast_to(cq_ref[...] * LOG2E, (tq, LANE))
    m_s[...] = jnp.full((tq, LANE), NEG_INF, F32)
    acc_s[...] = jnp.zeros((tq, 2 * HEAD_DIM), F32)
    rc = min(FOX_ROW_CHUNK, tq)

    def lanes(x, n):
        return jnp.concatenate([x] * (n // LANE), axis=1)

    def scores(kv, slot):
        start = pl.multiple_of(kv * tk, tk)
        s_s[slot] = _dot_nt(q, kb_ref[pl.ds(start, tk), :]) - ck_ref[kv] * LOG2E

    def update(kv, slot, diagonal):
        for r0 in range(0, tq, rc):
            rows = slice(r0, r0 + rc)
            s = s_s[slot, rows, :]
            if diagonal:
                row = lax.broadcasted_iota(jnp.int32, s.shape, 0) + r0
                col = lax.broadcasted_iota(jnp.int32, s.shape, 1)
                s = jnp.where(col <= row, s, NEG_INF)
            m_old = m_s[rows, :]
            m_new = jnp.maximum(m_old, jnp.max(s, axis=-1, keepdims=True) + cq_s[rows, :])
            p_s[rows, :] = jnp.exp2(s - lanes(m_new - cq_s[rows, :], tk)).astype(BF16)
            alpha_s[rows, :] = jnp.exp2(m_old - m_new)
            m_s[rows, :] = m_new
        start = pl.multiple_of(kv * tk, tk)
        pv = _dot(p_s[...], vb_ref[pl.ds(start, tk), :])
        acc_s[...] = lanes(alpha_s[...], 2 * HEAD_DIM) * acc_s[...] + pv

    def pair(jj, carry):
        scores(2 * jj + 1, 1)
        update(2 * jj, 0, False)
        scores(2 * jj + 2, 0)
        update(2 * jj + 1, 1, False)
        return carry

    scores(0, 0)
    lax.fori_loop(0, qi // 2, pair, 0)

    @pl.when(qi % 2 == 1)
    def _():
        scores(qi, 1)
        update(qi - 1, 0, False)
        update(qi, 1, True)

    @pl.when(qi % 2 == 0)
    def _():
        update(qi, 0, True)

    acc = acc_s[...]
    o = acc[:, :HEAD_DIM] / acc[:, HEAD_DIM:]
    o = o * lax.rsqrt(jnp.mean(o * o, axis=-1, keepdims=True) + EPS) * gain_ref[...]
    o_ref[...] = o.astype(o_ref.dtype)


def fox_call(z3, c, gain, batch, seq, tq=512):
    tq = min(tq, seq)
    tk = tq
    nq = seq // tq
    nk = seq // tk
    qb, kb, vb = (COL_BQKV // LANE, (COL_BQKV + GROUP_WIDTH) // LANE,
                  (COL_BQKV + 2 * GROUP_WIDTH) // LANE)
    c_col = c.reshape(batch, N_HEADS, seq, 1)
    c_row = c.reshape(batch, N_HEADS, nk, 1, tk)
    return pl.pallas_call(
        functools.partial(_fox_kernel, tq=tq, tk=tk),
        grid=(batch, N_HEADS, nq),
        in_specs=[pl.BlockSpec((None, tq, LANE), lambda b, h, i: (b, i, qb + h)),
                  pl.BlockSpec((None, seq, LANE), lambda b, h, i: (b, 0, kb + h)),
                  pl.BlockSpec((None, seq, LANE), lambda b, h, i: (b, 0, vb + h)),
                  pl.BlockSpec((None, None, tq, 1), lambda b, h, i: (b, h, i, 0)),
                  pl.BlockSpec((None, None, nk, 1, tk), lambda b, h, i: (b, h, 0, 0, 0)),
                  pl.BlockSpec((1, LANE), lambda b, h, i: (0, h))],
        out_specs=pl.BlockSpec((None, tq, LANE), lambda b, h, i: (b, i, h)),
        out_shape=jax.ShapeDtypeStruct((batch, seq, GROUP_WIDTH), BF16),
        scratch_shapes=[pltpu.VMEM((seq, LANE), BF16), pltpu.VMEM((seq, 2 * LANE), BF16),
                        pltpu.VMEM((2, tq, tk), F32), pltpu.VMEM((tq, tk), BF16),
                        pltpu.VMEM((tq, LANE), F32), pltpu.VMEM((tq, LANE), F32),
                        pltpu.VMEM((tq, LANE), F32), pltpu.VMEM((tq, 2 * HEAD_DIM), F32)],
        compiler_params=_params("parallel", "parallel", "arbitrary"),
        name="fox_attention",
    )(z3, z3, z3, c_col, c_row, gain)


def _l2norm(t):
    return t * lax.rsqrt(jnp.sum(t * t, axis=-1, keepdims=True) + EPS)


def _bmm(a, b):
    return lax.dot_general(a, b, (((2,), (1,)), ((0,), (0,))), preferred_element_type=F32)


def _bmm_nt(a, b):
    return lax.dot_general(a, b, (((2,), (2,)), ((0,), (0,))), preferred_element_type=F32)


def _bmm_tn(a, b):
    return lax.dot_general(a, b, (((1,), (1,)), ((0,), (0,))), preferred_element_type=F32)


def _neumann_tail(p):
    c = p.shape[-1]
    a = p
    q = p
    cover = 2
    while cover < c:
        qb = q.astype(BF16)
        q = _bmm(qb, qb)
        a = a + q + _bmm(a.astype(BF16), q.astype(BF16))
        cover *= 2
    return a


def _gdn_kernel(x_ref, zg_ref, s_ref, cw_ref, aneg_ref, dt_ref, gain_ref, o_ref,
                halo_ref, qkv_ref, state_ref):
    tt = x_ref.shape[0]
    c = GDN_CHUNK
    nc = tt // c
    w = GROUP_WIDTH
    hd = HEAD_DIM

    @pl.when(pl.program_id(1) == 0)
    def _():
        halo_ref[...] = jnp.zeros_like(halo_ref)
        state_ref[...] = jnp.zeros_like(state_ref)

    x = x_ref[...]
    qkv_ref[...] = _silu(_causal_conv(x, halo_ref[...], cw_ref[...], SHORT_CONV))
    halo_ref[...] = x[tt - SUBLANE:, :]
    small = s_ref[...]
    beta_all = _sigmoid(small)
    gc_all = _scan_add(aneg_ref[...] * _softplus(small + dt_ref[...]), c)

    pairs = [(ci, h) for ci in range(nc) for h in range(N_HEADS)]

    def gather(col0):
        return jnp.stack([qkv_ref[ci * c:(ci + 1) * c, col0 + h * hd:col0 + (h + 1) * hd]
                          for ci, h in pairs])

    q = _l2norm(gather(0)) * ATTN_SCALE
    k = _l2norm(gather(w))
    v = gather(2 * w)
    beta = jnp.stack([beta_all[ci * c:(ci + 1) * c, SMALL_BETA + h:SMALL_BETA + h + 1]
                      for ci, h in pairs])
    gc = jnp.stack([gc_all[ci * c:(ci + 1) * c, SMALL_ALPHA + h:SMALL_ALPHA + h + 1]
                    for ci, h in pairs])
    gc_t = [gc_all[ci * c:(ci + 1) * c, :].T for ci in range(nc)]
    gc_row = jnp.stack([gc_t[ci][SMALL_ALPHA + h:SMALL_ALPHA + h + 1, :]
                        for ci, h in pairs])

    row = lax.broadcasted_iota(jnp.int32, (c, c), 0)
    col = lax.broadcasted_iota(jnp.int32, (c, c), 1)
    tril = col <= row
    strict = col < row
    decay = jnp.where(tril, jnp.exp(jnp.where(tril, gc - gc_row, 0.0)), 0.0)
    kb = k.astype(BF16)
    kbeta = k * beta
    kk = _bmm_nt(kbeta.astype(BF16), kb) * decay
    tail = _neumann_tail(jnp.where(strict, -kk, 0.0))
    rhs = jnp.concatenate([v * beta, kbeta * jnp.exp(gc)], axis=-1)
    sol = rhs + _bmm(tail.astype(BF16), rhs.astype(BF16))
    u = sol[:, :, :hd]
    wy = sol[:, :, hd:]
    qk = jnp.where(tril, _bmm_nt(q.astype(BF16), kb) * decay, 0.0).astype(BF16)
    g_last = gc[:, c - 1:c, :]
    k_dec = (k * jnp.exp(g_last - gc)).astype(BF16)
    e_last = jnp.exp(g_last)
    wq = jnp.concatenate([wy.astype(BF16), (q * jnp.exp(gc)).astype(BF16)], axis=1)

    state = state_ref[...]
    for ci in range(nc):
        sl = slice(ci * N_HEADS, (ci + 1) * N_HEADS)
        ws = _bmm(wq[sl], state.astype(BF16))
        v_new = (u[sl] - ws[:, :c, :]).astype(BF16)
        o = ws[:, c:, :] + _bmm(qk[sl], v_new)
        state = state * e_last[sl] + _bmm_tn(k_dec[sl], v_new)
        o = o * lax.rsqrt(jnp.mean(o * o, axis=-1, keepdims=True) + EPS) * gain_ref[...]
        for h in range(N_HEADS):
            zg = zg_ref[ci * c:(ci + 1) * c, h * hd:(h + 1) * hd]
            o_ref[ci * c:(ci + 1) * c, h * hd:(h + 1) * hd] = (o[h] * _silu(zg)).astype(o_ref.dtype)
    state_ref[...] = state


def gdn_call(z, p, batch, seq, tt=512):
    tt = min(tt, seq)
    nt = seq // tt
    w = GROUP_WIDTH
    vec = pl.BlockSpec((1, LANE), lambda b, t: (0, 0))
    return pl.pallas_call(
        _gdn_kernel,
        grid=(batch, nt),
        in_specs=[pl.BlockSpec((tt, 3 * w), lambda b, t: (b * nt + t, COL_CQKV // (3 * w))),
                  pl.BlockSpec((tt, w), lambda b, t: (b * nt + t, COL_CZ // w)),
                  pl.BlockSpec((tt, LANE), lambda b, t: (b * nt + t, COL_SMALL // LANE)),
                  pl.BlockSpec((SHORT_CONV, 3 * w), lambda b, t: (0, 0)),
                  vec, vec, vec],
        out_specs=pl.BlockSpec((tt, w), lambda b, t: (b * nt + t, 0)),
        out_shape=jax.ShapeDtypeStruct((batch * seq, w), BF16),
        scratch_shapes=[pltpu.VMEM((SUBLANE, 3 * w), F32),
                        pltpu.VMEM((tt, 3 * w), F32),
                        pltpu.VMEM((N_HEADS, HEAD_DIM, HEAD_DIM), F32)],
        compiler_params=_params("parallel", "arbitrary"),
        name="gated_deltanet",
    )(z, z, z, p["gdn_conv_w"], p["gdn_aneg"], p["gdn_dt"], p["gdn_norm"])


def _dilated_kernel(q_ref, kc_ref, kp_ref, vc_ref, vp_ref, gain_ref, o_ref,
                    m_ref, acc_ref, l_ref):
    tt = q_ref.shape[0]
    first = pl.program_id(2) == 0
    i = lax.broadcasted_iota(jnp.int32, (SPAN, 2 * SPAN), 0)
    j = lax.broadcasted_iota(jnp.int32, (SPAN, 2 * SPAN), 1)
    band = jnp.logical_and(j >= i, j <= i + SPAN)
    band_first = jnp.logical_and(band, jnp.logical_or(j >= SPAN, jnp.logical_not(first)))
    ones = jnp.ones((2 * SPAN, HEAD_DIM), BF16)

    def rows(ref, start, n, dil):
        return ref[pl.ds(start, n, stride=dil), :] if dil > 1 else ref[pl.ds(start, n), :]

    for g, dil in enumerate(DILATIONS):
        for r in range(dil):
            for sb in range(tt // (dil * SPAN)):
                q0 = r + dil * SPAN * sb
                q = (rows(q_ref, q0, SPAN, dil) * (ATTN_SCALE * LOG2E)).astype(BF16)
                if sb == 0:
                    p0 = tt - dil * SPAN + r
                    k = jnp.concatenate([rows(kp_ref, p0, SPAN, dil), rows(kc_ref, r, SPAN, dil)], 0)
                    v = jnp.concatenate([rows(vp_ref, p0, SPAN, dil), rows(vc_ref, r, SPAN, dil)], 0)
                else:
                    k = rows(kc_ref, q0 - dil * SPAN, 2 * SPAN, dil)
                    v = rows(vc_ref, q0 - dil * SPAN, 2 * SPAN, dil)
                s = _dot_nt(q, k.astype(BF16))
                s = jnp.where(band_first if sb == 0 else band, s, NEG_INF)
                v1 = jnp.concatenate([v.astype(BF16), ones], axis=1)
                m_blk = jnp.broadcast_to(jnp.max(s, axis=-1, keepdims=True), (SPAN, LANE))
                if g == 0:
                    m_new = m_blk
                else:
                    m_old = rows(m_ref, q0, SPAN, dil)
                    m_new = jnp.maximum(m_old, m_blk)
                p = jnp.exp2(s - jnp.concatenate([m_new, m_new], axis=1))
                pv = _dot(p.astype(BF16), v1)
                acc, l = pv[:, :HEAD_DIM], pv[:, HEAD_DIM:]
                if g > 0:
                    a = jnp.exp2(m_old - m_new)
                    acc = acc + a * rows(acc_ref, q0, SPAN, dil)
                    l = l + a * rows(l_ref, q0, SPAN, dil)
                dst = pl.ds(q0, SPAN, stride=dil) if dil > 1 else pl.ds(q0, SPAN)
                m_ref[dst, :] = m_new
                acc_ref[dst, :] = acc
                l_ref[dst, :] = l
    y = acc_ref[...] / l_ref[...]
    y = y * lax.rsqrt(jnp.mean(y * y, axis=-1, keepdims=True) + EPS) * gain_ref[...]
    o_ref[...] = y.astype(o_ref.dtype)


def dilated_call(z3, gain, batch, seq):
    tt = max(DILATIONS) * SPAN
    assert seq % tt == 0
    nt = seq // tt
    qb, kb, vb = (COL_DQKV // LANE, (COL_DQKV + GROUP_WIDTH) // LANE,
                  (COL_DQKV + 2 * GROUP_WIDTH) // LANE)

    def cur(off):
        return pl.BlockSpec((None, tt, LANE), lambda b, h, i: (b, i, off + h))

    def prev(off):
        return pl.BlockSpec((None, tt, LANE), lambda b, h, i: (b, jnp.maximum(i - 1, 0), off + h))

    return pl.pallas_call(
        _dilated_kernel,
        grid=(batch, N_HEADS, nt),
        in_specs=[cur(qb), cur(kb), prev(kb), cur(vb), prev(vb),
                  pl.BlockSpec((1, LANE), lambda b, h, i: (0, h))],
        out_specs=pl.BlockSpec((None, tt, LANE), lambda b, h, i: (b, i, h)),
        out_shape=jax.ShapeDtypeStruct((batch, seq, GROUP_WIDTH), BF16),
        scratch_shapes=[pltpu.VMEM((tt, LANE), F32), pltpu.VMEM((tt, HEAD_DIM), F32),
                        pltpu.VMEM((tt, LANE), F32)],
        compiler_params=_params("parallel", "parallel", "arbitrary"),
        name="dilated_attention",
    )(z3, z3, z3, z3, z3, gain)


def _out_proj_kernel(ya, yb, yc, yd, w_ref, x_ref, g_ref, x1_ref, h_ref):
    w = GROUP_WIDTH
    acc = x_ref[...]
    for n, y in enumerate((ya, yb, yc, yd)):
        acc = acc + _dot(y[...], w_ref[n * w:(n + 1) * w, :])
    x1_ref[...] = acc
    h_ref[...] = _rmsnorm_rows(acc, g_ref[...]).astype(h_ref.dtype)


def out_proj_call(ys, w_out, layer, x, gain, tm=512):
    m, d = x.shape
    tm = min(tm, m)
    yblk = pl.BlockSpec((tm, GROUP_WIDTH), lambda i: (i, 0))
    xblk = pl.BlockSpec((tm, d), lambda i: (i, 0))
    return pl.pallas_call(
        _out_proj_kernel,
        grid=(m // tm,),
        in_specs=[yblk] * 4 + [pl.BlockSpec((None, d, d), lambda i: (layer, 0, 0),
                                            pipeline_mode=pl.Buffered(1)), xblk,
                               pl.BlockSpec((1, d), lambda i: (0, 0))],
        out_specs=[xblk, xblk],
        out_shape=[jax.ShapeDtypeStruct((m, d), F32), jax.ShapeDtypeStruct((m, d), BF16)],
        compiler_params=_params("parallel"),
        name="out_proj",
    )(*ys, w_out, x, gain)


FFN_SUB_ROWS = 128


def _ffn_up_kernel(h_ref, wu32_ref, wg32_ref, cwu_ref, cwg_ref, cbu_ref, cbg_ref, o_ref,
                   tail_u_ref, tail_g_ref, wu_ref, wg_ref, *, tiles_per_seq):
    tm = h_ref.shape[0]

    @pl.when(pl.program_id(1) == 0)
    def _():
        wu_ref[...] = wu32_ref[...].astype(BF16)
        wg_ref[...] = wg32_ref[...].astype(BF16)

    @pl.when(pl.program_id(1) % tiles_per_seq == 0)
    def _():
        tail_u_ref[...] = jnp.zeros_like(tail_u_ref)
        tail_g_ref[...] = jnp.zeros_like(tail_g_ref)

    sub = min(FFN_SUB_ROWS, tm)
    tail_u = tail_u_ref[...]
    tail_g = tail_g_ref[...]
    for r0 in range(0, tm, sub):
        h = h_ref[r0:r0 + sub, :]
        u = _dot(h, wu_ref[...])
        g = _dot(h, wg_ref[...])
        up = _causal_conv(u, tail_u, cwu_ref[...], FFN_CONV) + cbu_ref[...]
        gate = _causal_conv(g, tail_g, cwg_ref[...], FFN_CONV) + cbg_ref[...]
        tail_u = u[sub - SUBLANE:, :]
        tail_g = g[sub - SUBLANE:, :]
        o_ref[r0:r0 + sub, :] = (_silu(gate) * up).astype(o_ref.dtype)
    tail_u_ref[...] = tail_u
    tail_g_ref[...] = tail_g


def ffn_up_call(h, w_up, layer, conv_w, conv_b, seq, tm=1024, tn=512):
    m, d = h.shape
    tm = min(tm, seq)
    nj = D_FF // tn
    return pl.pallas_call(
        functools.partial(_ffn_up_kernel, tiles_per_seq=seq // tm),
        grid=(nj, m // tm),
        in_specs=[pl.BlockSpec((tm, d), lambda j, i: (i, 0)),
                  pl.BlockSpec((None, d, tn), lambda j, i: (layer, 0, j)),
                  pl.BlockSpec((None, d, tn), lambda j, i: (layer, 0, j + nj)),
                  pl.BlockSpec((FFN_CONV, tn), lambda j, i: (0, j)),
                  pl.BlockSpec((FFN_CONV, tn), lambda j, i: (0, j + nj)),
                  pl.BlockSpec((1, tn), lambda j, i: (0, j)),
                  pl.BlockSpec((1, tn), lambda j, i: (0, j + nj))],
        out_specs=pl.BlockSpec((tm, tn), lambda j, i: (i, j)),
        out_shape=jax.ShapeDtypeStruct((m, D_FF), BF16),
        scratch_shapes=[pltpu.VMEM((SUBLANE, tn), F32), pltpu.VMEM((SUBLANE, tn), F32),
                        pltpu.VMEM((d, tn), BF16), pltpu.VMEM((d, tn), BF16)],
        compiler_params=_params("parallel", "arbitrary"),
        name="ffn_up",
    )(h, w_up, w_up, conv_w, conv_w, conv_b, conv_b)


def _ffn_down_kernel(g_ref, w_ref, x_ref, gain_ref, x2_ref, hn_ref):
    x2 = x_ref[...] + _dot(g_ref[...], w_ref[...])
    x2_ref[...] = x2
    hn_ref[...] = _rmsnorm_rows(x2, gain_ref[...]).astype(hn_ref.dtype)


def ffn_down_call(g, w_down, layer, x, gain, hn_dtype, tm=256):
    m, d = x.shape
    tm = min(tm, m)
    kdim = g.shape[1]
    xblk = pl.BlockSpec((tm, d), lambda i: (i, 0))
    return pl.pallas_call(
        _ffn_down_kernel,
        grid=(m // tm,),
        in_specs=[pl.BlockSpec((tm, kdim), lambda i: (i, 0)),
                  pl.BlockSpec((None, kdim, d), lambda i: (layer, 0, 0),
                               pipeline_mode=pl.Buffered(1)),
                  xblk, pl.BlockSpec((1, d), lambda i: (0, 0))],
        out_specs=[xblk, xblk],
        out_shape=[jax.ShapeDtypeStruct((m, d), F32), jax.ShapeDtypeStruct((m, d), hn_dtype)],
        compiler_params=_params("parallel"),
        name="ffn_down",
    )(g, w_down, x, gain)


def _block_diag(blocks):
    n, r, c = blocks.shape
    eye = jnp.eye(n, dtype=blocks.dtype)
    return (eye[:, None, :, None] * blocks[:, :, None, :]).reshape(n * r, n * c)


def _small_vec(values, offset):
    return jnp.zeros((1, LANE), F32).at[0, offset:offset + values.shape[0]].set(values)


def _w_in_segments():
    gw = GROUP_WIDTH
    sizes = (gw, gw, 3 * gw, N_HEADS, 3 * gw, gw, N_HEADS, N_HEADS, 3 * gw)
    dsts = (COL_AX, COL_AG, COL_BQKV, COL_SMALL + SMALL_F, COL_CQKV, COL_CZ,
            COL_SMALL + SMALL_BETA, COL_SMALL + SMALL_ALPHA, COL_DQKV)
    segs, src = [], 0
    for size, dst in zip(sizes, dsts):
        segs.append((src, dst, size))
        src += size
    return segs, src


def _prep_w_in_kernel(w_ref, o_ref):
    segs, _ = _w_in_segments()
    o_ref[:, COL_SMALL:] = jnp.zeros((o_ref.shape[0], Z_COLS - COL_SMALL), o_ref.dtype)
    for src, dst, size in segs:
        lo = (src // LANE) * LANE
        hi = min(-(-(src + size) // LANE) * LANE, w_ref.shape[1])
        tile = w_ref[:, lo:hi]
        o_ref[:, dst:dst + size] = tile[:, src - lo:src - lo + size].astype(o_ref.dtype)


def prep_w_in_call(w_in, tk=256):
    depth, k, n = w_in.shape
    assert n == _w_in_segments()[1]
    return pl.pallas_call(
        _prep_w_in_kernel,
        grid=(depth, k // tk),
        in_specs=[pl.BlockSpec((None, tk, n), lambda l, i: (l, i, 0))],
        out_specs=pl.BlockSpec((None, tk, Z_COLS), lambda l, i: (l, i, 0)),
        out_shape=jax.ShapeDtypeStruct((depth, k, Z_COLS), BF16),
        compiler_params=_params("parallel", "parallel"),
        name="prep_w_in",
    )(w_in)


def kernel(x, norm_mix, w_in, lru_conv_w, lru_conv_b, lru_wa, lru_ba, lru_wx, lru_bx,
           lru_lambda, fox_f_bias, gdn_conv_w, gdn_a_log, gdn_dt_bias, gdn_norm,
           norm_a, norm_b, norm_d, w_out, norm_ffn, ffn_w_up, ffn_conv_w, ffn_conv_b,
           ffn_w_down, norm_final):
    batch, seq, d = x.shape
    depth = w_in.shape[0]
    m = batch * seq
    gw = GROUP_WIDTH
    gmat64 = _block_diag(jnp.full((LRU_BLOCKS, LRU_BLOCK_DIM, LRU_BLOCK_DIM),
                                  1.0 / LRU_BLOCK_DIM, F32)).astype(BF16)
    xs = x.reshape(m, d)
    w_in_z = prep_w_in_call(w_in)
    w_out_b = w_out.astype(BF16)
    w_down_b = ffn_w_down.astype(BF16)
    h = rmsnorm_call(xs, norm_mix[0], BF16)
    for l in range(depth):
        p = {
            "lru_conv_w": lru_conv_w[l], "lru_conv_b": lru_conv_b[l].reshape(1, gw),
            "lru_wa": _block_diag(lru_wa[l]).astype(BF16), "lru_ba": lru_ba[l].reshape(1, gw),
            "lru_wx": _block_diag(lru_wx[l]).astype(BF16), "lru_bx": lru_bx[l].reshape(1, gw),
            "lru_lambda": lru_lambda[l].reshape(1, gw), "gmat64": gmat64,
            "norm_a": norm_a[l].reshape(1, gw),
            "gdn_conv_w": gdn_conv_w[l],
            "gdn_aneg": _small_vec(-jnp.exp(gdn_a_log[l]), SMALL_ALPHA),
            "gdn_dt": _small_vec(gdn_dt_bias[l], SMALL_ALPHA),
            "gdn_norm": gdn_norm[l].reshape(1, HEAD_DIM),
        }
        z = in_proj_call(h, w_in_z, l)

        y_a = lru_call(z, p, batch, seq)

        c = logf_cumsum_call(z, _small_vec(fox_f_bias[l], SMALL_F), batch, seq)
        c = c.reshape(batch, seq, LANE)[:, :, SMALL_F:SMALL_F + N_HEADS]
        c = jnp.transpose(c, (0, 2, 1))
        z3 = z.reshape(batch, seq, Z_COLS)
        y_b = fox_call(z3, c, norm_b[l].reshape(1, gw), batch, seq).reshape(m, gw)

        y_c = gdn_call(z, p, batch, seq)

        y_d = dilated_call(z3, norm_d[l].reshape(1, gw), batch, seq).reshape(m, gw)

        x1, hf = out_proj_call((y_a, y_b, y_c, y_d), w_out_b, l, xs,
                               norm_ffn[l].reshape(1, d))
        g = ffn_up_call(hf, ffn_w_up, l, ffn_conv_w[l], ffn_conv_b[l].reshape(1, 2 * D_FF), seq)
        last = l == depth - 1
        gain_next = norm_final if last else norm_mix[l + 1]
        xs, h = ffn_down_call(g, w_down_b, l, x1, gain_next.reshape(1, d),
                              F32 if last else BF16)
    return h.reshape(batch, seq, d)
```

```python
import functools
import math

import jax
import jax.numpy as jnp
from jax import lax
from jax.experimental import pallas as pl
from jax.experimental.pallas import tpu as pltpu

F32 = jnp.float32
BF16 = jnp.bfloat16

D_MODEL = 2048
GROUP_WIDTH = 512
HEAD_DIM = 128
N_HEADS = 4
LRU_BLOCKS = 8
LRU_BLOCK_DIM = 64
LRU_C = 8.0
SHORT_CONV = 4
FFN_CONV = 3
D_FF = 5632
GDN_CHUNK = 64
SPAN = 128
DILATIONS = (1, 4, 16)
EPS = 1e-6
NEG_INF = -1e30
ATTN_SCALE = HEAD_DIM ** -0.5
LOG2E = math.log2(math.e)

LANE = 128
SUBLANE = 8
VMEM_LIMIT = 52 * 1024 * 1024

COL_CQKV = 0
COL_AX = 1536
COL_AG = 2048
COL_CZ = 2560
COL_BQKV = 3072
COL_DQKV = 4608
COL_SMALL = 6144
Z_COLS = 6400
SMALL_F, SMALL_BETA, SMALL_ALPHA = 0, 4, 8


def _params(*sem):
    return pltpu.CompilerParams(dimension_semantics=sem, vmem_limit_bytes=VMEM_LIMIT)


def _sigmoid(x):
    return 1.0 / (1.0 + jnp.exp(-x))


def _softplus(x):
    return jnp.maximum(x, 0.0) + jnp.log1p(jnp.exp(-jnp.abs(x)))


def _silu(x):
    return x * _sigmoid(x)


def _gelu_tanh(x):
    c = math.sqrt(2.0 / math.pi)
    return 0.5 * x * (1.0 + jnp.tanh(c * (x + 0.044715 * (x * x * x))))


def _dot(a, b, precision=None):
    return jnp.dot(a, b, preferred_element_type=F32, precision=precision)


def _dot_nt(a, b):
    return lax.dot_general(a, b, (((1,), (1,)), ((), ())), preferred_element_type=F32)


def _dot_tn(a, b):
    return lax.dot_general(a, b, (((0,), (0,)), ((), ())), preferred_element_type=F32)


def _causal_conv(x, halo, w, taps):
    n = x.shape[0]
    cat = jnp.concatenate([halo, x], axis=0)
    y = w[taps - 1:taps, :] * x
    for k in range(taps - 1):
        shifted = pltpu.roll(cat, taps - 1 - k, 0)[SUBLANE:SUBLANE + n, :]
        y = y + w[k:k + 1, :] * shifted
    return y


def _scan_affine(a, u, h0):
    n, c = a.shape
    groups = n // SUBLANE
    a = a.reshape(groups, SUBLANE, c)
    u = u.reshape(groups, SUBLANE, c)
    pos = lax.broadcasted_iota(jnp.int32, a.shape, 1)
    s = 1
    while s < SUBLANE:
        keep = pos >= s
        a_s = jnp.where(keep, pltpu.roll(a, s, 1), 1.0)
        u_s = jnp.where(keep, pltpu.roll(u, s, 1), 0.0)
        u = a * u_s + u
        a = a * a_s
        s *= 2
    out = []
    carry = h0
    for g in range(groups):
        hg = u[g] + a[g] * carry
        carry = hg[SUBLANE - 1:SUBLANE, :]
        out.append(hg)
    return jnp.concatenate(out, axis=0)


def _scan_add(x, seg):
    row = lax.broadcasted_iota(jnp.int32, x.shape, 0)
    pos = row % seg
    s = 1
    while s < seg:
        x = x + jnp.where(pos >= s, pltpu.roll(x, s, 0), 0.0)
        s *= 2
    return x


def _rmsnorm_rows(x, gain):
    ms = jnp.mean(x * x, axis=-1, keepdims=True)
    return x * lax.rsqrt(ms + EPS) * gain


def _rmsnorm_kernel(x_ref, g_ref, o_ref):
    o_ref[...] = _rmsnorm_rows(x_ref[...], g_ref[...]).astype(o_ref.dtype)


def rmsnorm_call(x, gain, out_dtype, tm=512):
    m, d = x.shape
    return pl.pallas_call(
        _rmsnorm_kernel,
        grid=(m // tm,),
        in_specs=[pl.BlockSpec((tm, d), lambda i: (i, 0)),
                  pl.BlockSpec((1, d), lambda i: (0, 0))],
        out_specs=pl.BlockSpec((tm, d), lambda i: (i, 0)),
        out_shape=jax.ShapeDtypeStruct((m, d), out_dtype),
        compiler_params=_params("parallel"),
        name="rmsnorm",
    )(x, gain.reshape(1, d))


def _matmul_kernel(a_ref, b_ref, o_ref):
    o_ref[...] = _dot(a_ref[...], b_ref[...])


def in_proj_call(h, w, layer, tm=1024, tn=1280):
    m, k = h.shape
    n = w.shape[2]
    tm = min(tm, m)
    return pl.pallas_call(
        _matmul_kernel,
        grid=(m // tm, n // tn),
        in_specs=[pl.BlockSpec((tm, k), lambda i, j: (i, 0)),
                  pl.BlockSpec((None, k, tn), lambda i, j: (layer, 0, j))],
        out_specs=pl.BlockSpec((tm, tn), lambda i, j: (i, j)),
        out_shape=jax.ShapeDtypeStruct((m, n), F32),
        compiler_params=_params("parallel", "arbitrary"),
        name="in_proj",
    )(h, w)


def _group_mean_sq(h, gmat):
    h2 = h * h
    hi = h2.astype(BF16)
    lo = (h2 - hi.astype(F32)).astype(BF16)
    return _dot(hi, gmat) + _dot(lo, gmat)


def _lru_kernel(ax_ref, ag_ref, cw_ref, cb_ref, wa_ref, ba_ref, wx_ref, bx_ref, lam_ref,
                gmat_ref, gain_ref, o_ref, halo_ref, hprev_ref):
    tt = ax_ref.shape[0]

    @pl.when(pl.program_id(1) == 0)
    def _():
        halo_ref[...] = jnp.zeros_like(halo_ref)
        hprev_ref[...] = jnp.zeros_like(hprev_ref)

    x = ax_ref[...]
    xc = _causal_conv(x, halo_ref[...], cw_ref[...], SHORT_CONV) + cb_ref[...]
    halo_ref[...] = x[tt - SUBLANE:, :]
    xb = xc.astype(BF16)
    r = _sigmoid(_dot(xb, wa_ref[...]) + ba_ref[...])
    i = _sigmoid(_dot(xb, wx_ref[...]) + bx_ref[...])
    log_a = (-LRU_C) * r * _softplus(-lam_ref[...])
    a = jnp.exp(log_a)
    one_m_a2 = -jnp.tanh(log_a) * (a * a + 1.0)
    root = jnp.where(one_m_a2 > 0.0, one_m_a2 * lax.rsqrt(one_m_a2), 0.0)
    u = root * (i * xc)
    h = _scan_affine(a, u, hprev_ref[...])
    hprev_ref[...] = h[tt - 1:tt, :]
    y = h * lax.rsqrt(_group_mean_sq(h, gmat_ref[...]) + EPS) * gain_ref[...]
    o_ref[...] = (y * _gelu_tanh(ag_ref[...])).astype(o_ref.dtype)


def lru_call(z, p, batch, seq, tt=512):
    tt = min(tt, seq)
    nt = seq // tt
    w = GROUP_WIDTH
    row = lambda b, t: (b * nt + t)
    vec = pl.BlockSpec((1, w), lambda b, t: (0, 0))
    mat = pl.BlockSpec((w, w), lambda b, t: (0, 0))
    return pl.pallas_call(
        _lru_kernel,
        grid=(batch, nt),
        in_specs=[pl.BlockSpec((tt, w), lambda b, t: (row(b, t), COL_AX // w)),
                  pl.BlockSpec((tt, w), lambda b, t: (row(b, t), COL_AG // w)),
                  pl.BlockSpec((SHORT_CONV, w), lambda b, t: (0, 0)),
                  vec, mat, vec, mat, vec, vec, mat, vec],
        out_specs=pl.BlockSpec((tt, w), lambda b, t: (row(b, t), 0)),
        out_shape=jax.ShapeDtypeStruct((batch * seq, w), BF16),
        scratch_shapes=[pltpu.VMEM((SUBLANE, w), F32), pltpu.VMEM((1, w), F32)],
        compiler_params=_params("parallel", "arbitrary"),
        name="rg_lru",
    )(z, z, p["lru_conv_w"], p["lru_conv_b"], p["lru_wa"], p["lru_ba"], p["lru_wx"],
      p["lru_bx"], p["lru_lambda"], p["gmat64"], p["norm_a"])


def _logf_cumsum_kernel(s_ref, bias_ref, o_ref, carry_ref):
    tt = s_ref.shape[0]

    @pl.when(pl.program_id(1) == 0)
    def _():
        carry_ref[...] = jnp.zeros_like(carry_ref)

    log_f = -_softplus(-(s_ref[...] + bias_ref[...]))
    c = _scan_add(log_f, tt) + carry_ref[...]
    carry_ref[...] = c[tt - 1:tt, :]
    o_ref[...] = c


def logf_cumsum_call(z, bias, batch, seq, tt=1024):
    tt = min(tt, seq)
    nt = seq // tt
    return pl.pallas_call(
        _logf_cumsum_kernel,
        grid=(batch, nt),
        in_specs=[pl.BlockSpec((tt, LANE), lambda b, t: (b * nt + t, COL_SMALL // LANE)),
                  pl.BlockSpec((1, LANE), lambda b, t: (0, 0))],
        out_specs=pl.BlockSpec((tt, LANE), lambda b, t: (b * nt + t, 0)),
        out_shape=jax.ShapeDtypeStruct((batch * seq, LANE), F32),
        scratch_shapes=[pltpu.VMEM((1, LANE), F32)],
        compiler_params=_params("parallel", "arbitrary"),
        name="logf_cumsum",
    )(z, bias)


FOX_ROW_CHUNK = 64


def _fox_kernel(q_ref, k_ref, v_ref, cq_ref, ck_ref, gain_ref, o_ref, kb_ref, vb_ref,
                s_s, p_s, m_s, alpha_s, cq_s, acc_s, *, tq, tk):
    qi = pl.program_id(2)

    @pl.when(qi == 0)
    def _():
        kb_ref[...] = k_ref[...].astype(BF16)
        vb_ref[:, :HEAD_DIM] = v_ref[...].astype(BF16)
        vb_ref[:, HEAD_DIM:] = jnp.ones((vb_ref.shape[0], HEAD_DIM), BF16)

    q = (q_ref[...] * (ATTN_SCALE * LOG2E)).astype(BF16)
    cq_s[...] = jnp.broadcast_to(cq_ref[...] * LOG2E, (tq, LANE))
    m_s[...] = jnp.full((tq, LANE), NEG_INF, F32)
    acc_s[...] = jnp.zeros((tq, 2 * HEAD_DIM), F32)
    rc = min(FOX_ROW_CHUNK, tq)

    def lanes(x, n):
        return jnp.concatenate([x] * (n // LANE), axis=1)

    def scores(kv, slot):
        start = pl.multiple_of(kv * tk, tk)
        s_s[slot] = _dot_nt(q, kb_ref[pl.ds(start, tk), :]) - ck_ref[kv] * LOG2E

    def update(kv, slot, diagonal):
        for r0 in range(0, tq, rc):
            rows = slice(r0, r0 + rc)
            s = s_s[slot, rows, :]
            if diagonal:
                row = lax.broadcasted_iota(jnp.int32, s.shape, 0) + r0
                col = lax.broadcasted_iota(jnp.int32, s.shape, 1)
                s = jnp.where(col <= row, s, NEG_INF)
            m_old = m_s[rows, :]
            m_new = jnp.maximum(m_old, jnp.max(s, axis=-1, keepdims=True) + cq_s[rows, :])
            p_s[rows, :] = jnp.exp2(s - lanes(m_new - cq_s[rows, :], tk)).astype(BF16)
            alpha_s[rows, :] = jnp.exp2(m_old - m_new)
            m_s[rows, :] = m_new
        start = pl.multiple_of(kv * tk, tk)
        pv = _dot(p_s[...], vb_ref[pl.ds(start, tk), :])
        acc_s[...] = lanes(alpha_s[...], 2 * HEAD_DIM) * acc_s[...] + pv

    def pair(jj, carry):
        scores(2 * jj + 1, 1)
        update(2 * jj, 0, False)
        scores(2 * jj + 2, 0)
        update(2 * jj + 1, 1, False)
        return carry

    scores(0, 0)
    lax.fori_loop(0, qi // 2, pair, 0)

    @pl.when(qi % 2 == 1)
    def _():
        scores(qi, 1)
        update(qi - 1, 0, False)
        update(qi, 1, True)

    @pl.when(qi % 2 == 0)
    def _():
        update(qi, 0, True)

    acc = acc_s[...]
    o = acc[:, :HEAD_DIM] / acc[:, HEAD_DIM:]
    o = o * lax.rsqrt(jnp.mean(o * o, axis=-1, keepdims=True) + EPS) * gain_ref[...]
    o_ref[...] = o.astype(o_ref.dtype)


def fox_call(z3, c, gain, batch, seq, tq=512):
    tq = min(tq, seq)
    tk = tq
    nq = seq // tq
    nk = seq // tk
    qb, kb, vb = (COL_BQKV // LANE, (COL_BQKV + GROUP_WIDTH) // LANE,
                  (COL_BQKV + 2 * GROUP_WIDTH) // LANE)
    c_col = c.reshape(batch, N_HEADS, seq, 1)
    c_row = c.reshape(batch, N_HEADS, nk, 1, tk)
    return pl.pallas_call(
        functools.partial(_fox_kernel, tq=tq, tk=tk),
        grid=(batch, N_HEADS, nq),
        in_specs=[pl.BlockSpec((None, tq, LANE), lambda b, h, i: (b, i, qb + h)),
                  pl.BlockSpec((None, seq, LANE), lambda b, h, i: (b, 0, kb + h)),
                  pl.BlockSpec((None, seq, LANE), lambda b, h, i: (b, 0, vb + h)),
                  pl.BlockSpec((None, None, tq, 1), lambda b, h, i: (b, h, i, 0)),
                  pl.BlockSpec((None, None, nk, 1, tk), lambda b, h, i: (b, h, 0, 0, 0)),
                  pl.BlockSpec((1, LANE), lambda b, h, i: (0, h))],
        out_specs=pl.BlockSpec((None, tq, LANE), lambda b, h, i: (b, i, h)),
        out_shape=jax.ShapeDtypeStruct((batch, seq, GROUP_WIDTH), BF16),
        scratch_shapes=[pltpu.VMEM((seq, LANE), BF16), pltpu.VMEM((seq, 2 * LANE), BF16),
                        pltpu.VMEM((2, tq, tk), F32), pltpu.VMEM((tq, tk), BF16),
                        pltpu.VMEM((tq, LANE), F32), pltpu.VMEM((tq, LANE), F32),
                        pltpu.VMEM((tq, LANE), F32), pltpu.VMEM((tq, 2 * HEAD_DIM), F32)],
        compiler_params=_params("parallel", "parallel", "arbitrary"),
        name="fox_attention",
    )(z3, z3, z3, c_col, c_row, gain)


def _l2norm(t):
    return t * lax.rsqrt(jnp.sum(t * t, axis=-1, keepdims=True) + EPS)


def _bmm(a, b):
    return lax.dot_general(a, b, (((2,), (1,)), ((0,), (0,))), preferred_element_type=F32)


def _bmm_nt(a, b):
    return lax.dot_general(a, b, (((2,), (2,)), ((0,), (0,))), preferred_element_type=F32)


def _bmm_tn(a, b):
    return lax.dot_general(a, b, (((1,), (1,)), ((0,), (0,))), preferred_element_type=F32)


def _neumann_tail(p):
    c = p.shape[-1]
    a = p
    q = p
    cover = 2
    while cover < c:
        qb = q.astype(BF16)
        q = _bmm(qb, qb)
        a = a + q + _bmm(a.astype(BF16), q.astype(BF16))
        cover *= 2
    return a


def _gdn_kernel(x_ref, zg_ref, s_ref, cw_ref, aneg_ref, dt_ref, gain_ref, o_ref,
                halo_ref, qkv_ref, state_ref):
    tt = x_ref.shape[0]
    c = GDN_CHUNK
    nc = tt // c
    w = GROUP_WIDTH
    hd = HEAD_DIM

    @pl.when(pl.program_id(1) == 0)
    def _():
        halo_ref[...] = jnp.zeros_like(halo_ref)
        state_ref[...] = jnp.zeros_like(state_ref)

    x = x_ref[...]
    qkv_ref[...] = _silu(_causal_conv(x, halo_ref[...], cw_ref[...], SHORT_CONV))
    halo_ref[...] = x[tt - SUBLANE:, :]
    small = s_ref[...]
    beta_all = _sigmoid(small)
    gc_all = _scan_add(aneg_ref[...] * _softplus(small + dt_ref[...]), c)

    pairs = [(ci, h) for ci in range(nc) for h in range(N_HEADS)]

    def gather(col0):
        return jnp.stack([qkv_ref[ci * c:(ci + 1) * c, col0 + h * hd:col0 + (h + 1) * hd]
                          for ci, h in pairs])

    q = _l2norm(gather(0)) * ATTN_SCALE
    k = _l2norm(gather(w))
    v = gather(2 * w)
    beta = jnp.stack([beta_all[ci * c:(ci + 1) * c, SMALL_BETA + h:SMALL_BETA + h + 1]
                      for ci, h in pairs])
    gc = jnp.stack([gc_all[ci * c:(ci + 1) * c, SMALL_ALPHA + h:SMALL_ALPHA + h + 1]
                    for ci, h in pairs])
    gc_t = [gc_all[ci * c:(ci + 1) * c, :].T for ci in range(nc)]
    gc_row = jnp.stack([gc_t[ci][SMALL_ALPHA + h:SMALL_ALPHA + h + 1, :]
                        for ci, h in pairs])

    row = lax.broadcasted_iota(jnp.int32, (c, c), 0)
    col = lax.broadcasted_iota(jnp.int32, (c, c), 1)
    tril = col <= row
    strict = col < row
    decay = jnp.where(tril, jnp.exp(jnp.where(tril, gc - gc_row, 0.0)), 0.0)
    kb = k.astype(BF16)
    kbeta = k * beta
    kk = _bmm_nt(kbeta.astype(BF16), kb) * decay
    tail = _neumann_tail(jnp.where(strict, -kk, 0.0))
    rhs = jnp.concatenate([v * beta, kbeta * jnp.exp(gc)], axis=-1)
    sol = rhs + _bmm(tail.astype(BF16), rhs.astype(BF16))
    u = sol[:, :, :hd]
    wy = sol[:, :, hd:]
    qk = jnp.where(tril, _bmm_nt(q.astype(BF16), kb) * decay, 0.0).astype(BF16)
    g_last = gc[:, c - 1:c, :]
    k_dec = (k * jnp.exp(g_last - gc)).astype(BF16)
    e_last = jnp.exp(g_last)
    wq = jnp.concatenate([wy.astype(BF16), (q * jnp.exp(gc)).astype(BF16)], axis=1)

    state = state_ref[...]
    for ci in range(nc):
        sl = slice(ci * N_HEADS, (ci + 1) * N_HEADS)
        ws = _bmm(wq[sl], state.astype(BF16))
        v_new = (u[sl] - ws[:, :c, :]).astype(BF16)
        o = ws[:, c:, :] + _bmm(qk[sl], v_new)
        state = state * e_last[sl] + _bmm_tn(k_dec[sl], v_new)
        o = o * lax.rsqrt(jnp.mean(o * o, axis=-1, keepdims=True) + EPS) * gain_ref[...]
        for h in range(N_HEADS):
            zg = zg_ref[ci * c:(ci + 1) * c, h * hd:(h + 1) * hd]
            o_ref[ci * c:(ci + 1) * c, h * hd:(h + 1) * hd] = (o[h] * _silu(zg)).astype(o_ref.dtype)
    state_ref[...] = state


def gdn_call(z, p, batch, seq, tt=512):
    tt = min(tt, seq)
    nt = seq // tt
    w = GROUP_WIDTH
    vec = pl.BlockSpec((1, LANE), lambda b, t: (0, 0))
    return pl.pallas_call(
        _gdn_kernel,
        grid=(batch, nt),
        in_specs=[pl.BlockSpec((tt, 3 * w), lambda b, t: (b * nt + t, COL_CQKV // (3 * w))),
                  pl.BlockSpec((tt, w), lambda b, t: (b * nt + t, COL_CZ // w)),
                  pl.BlockSpec((tt, LANE), lambda b, t: (b * nt + t, COL_SMALL // LANE)),
                  pl.BlockSpec((SHORT_CONV, 3 * w), lambda b, t: (0, 0)),
                  vec, vec, vec],
        out_specs=pl.BlockSpec((tt, w), lambda b, t: (b * nt + t, 0)),
        out_shape=jax.ShapeDtypeStruct((batch * seq, w), BF16),
        scratch_shapes=[pltpu.VMEM((SUBLANE, 3 * w), F32),
                        pltpu.VMEM((tt, 3 * w), F32),
                        pltpu.VMEM((N_HEADS, HEAD_DIM, HEAD_DIM), F32)],
        compiler_params=_params("parallel", "arbitrary"),
        name="gated_deltanet",
    )(z, z, z, p["gdn_conv_w"], p["gdn_aneg"], p["gdn_dt"], p["gdn_norm"])


def _dilated_kernel(q_ref, kc_ref, kp_ref, vc_ref, vp_ref, gain_ref, o_ref,
                    m_ref, acc_ref, l_ref):
    tt = q_ref.shape[0]
    first = pl.program_id(2) == 0
    i = lax.broadcasted_iota(jnp.int32, (SPAN, 2 * SPAN), 0)
    j = lax.broadcasted_iota(jnp.int32, (SPAN, 2 * SPAN), 1)
    band = jnp.logical_and(j >= i, j <= i + SPAN)
    band_first = jnp.logical_and(band, jnp.logical_or(j >= SPAN, jnp.logical_not(first)))
    ones = jnp.ones((2 * SPAN, HEAD_DIM), BF16)

    def rows(ref, start, n, dil):
        return ref[pl.ds(start, n, stride=dil), :] if dil > 1 else ref[pl.ds(start, n), :]

    for g, dil in enumerate(DILATIONS):
        for r in range(dil):
            for sb in range(tt // (dil * SPAN)):
                q0 = r + dil * SPAN * sb
                q = (rows(q_ref, q0, SPAN, dil) * (ATTN_SCALE * LOG2E)).astype(BF16)
                if sb == 0:
                    p0 = tt - dil * SPAN + r
                    k = jnp.concatenate([rows(kp_ref, p0, SPAN, dil), rows(kc_ref, r, SPAN, dil)], 0)
                    v = jnp.concatenate([rows(vp_ref, p0, SPAN, dil), rows(vc_ref, r, SPAN, dil)], 0)
                else:
                    k = rows(kc_ref, q0 - dil * SPAN, 2 * SPAN, dil)
                    v = rows(vc_ref, q0 - dil * SPAN, 2 * SPAN, dil)
                s = _dot_nt(q, k.astype(BF16))
                s = jnp.where(band_first if sb == 0 else band, s, NEG_INF)
                v1 = jnp.concatenate([v.astype(BF16), ones], axis=1)
                m_blk = jnp.broadcast_to(jnp.max(s, axis=-1, keepdims=True), (SPAN, LANE))
                if g == 0:
                    m_new = m_blk
                else:
                    m_old = rows(m_ref, q0, SPAN, dil)
                    m_new = jnp.maximum(m_old, m_blk)
                p = jnp.exp2(s - jnp.concatenate([m_new, m_new], axis=1))
                pv = _dot(p.astype(BF16), v1)
                acc, l = pv[:, :HEAD_DIM], pv[:, HEAD_DIM:]
                if g > 0:
                    a = jnp.exp2(m_old - m_new)
                    acc = acc + a * rows(acc_ref, q0, SPAN, dil)
                    l = l + a * rows(l_ref, q0, SPAN, dil)
                dst = pl.ds(q0, SPAN, stride=dil) if dil > 1 else pl.ds(q0, SPAN)
                m_ref[dst, :] = m_new
                acc_ref[dst, :] = acc
                l_ref[dst, :] = l
    y = acc_ref[...] / l_ref[...]
    y = y * lax.rsqrt(jnp.mean(y * y, axis=-1, keepdims=True) + EPS) * gain_ref[...]
    o_ref[...] = y.astype(o_ref.dtype)


def dilated_call(z3, gain, batch, seq):
    tt = max(DILATIONS) * SPAN
    assert seq % tt == 0
    nt = seq // tt
    qb, kb, vb = (COL_DQKV // LANE, (COL_DQKV + GROUP_WIDTH) // LANE,
                  (COL_DQKV + 2 * GROUP_WIDTH) // LANE)

    def cur(off):
        return pl.BlockSpec((None, tt, LANE), lambda b, h, i: (b, i, off + h))

    def prev(off):
        return pl.BlockSpec((None, tt, LANE), lambda b, h, i: (b, jnp.maximum(i - 1, 0), off + h))

    return pl.pallas_call(
        _dilated_kernel,
        grid=(batch, N_HEADS, nt),
        in_specs=[cur(qb), cur(kb), prev(kb), cur(vb), prev(vb),
                  pl.BlockSpec((1, LANE), lambda b, h, i: (0, h))],
        out_specs=pl.BlockSpec((None, tt, LANE), lambda b, h, i: (b, i, h)),
        out_shape=jax.ShapeDtypeStruct((batch, seq, GROUP_WIDTH), BF16),
        scratch_shapes=[pltpu.VMEM((tt, LANE), F32), pltpu.VMEM((tt, HEAD_DIM), F32),
                        pltpu.VMEM((tt, LANE), F32)],
        compiler_params=_params("parallel", "parallel", "arbitrary"),
        name="dilated_attention",
    )(z3, z3, z3, z3, z3, gain)


def _out_proj_kernel(ya, yb, yc, yd, w_ref, x_ref, g_ref, x1_ref, h_ref):
    w = GROUP_WIDTH
    acc = x_ref[...]
    for n, y in enumerate((ya, yb, yc, yd)):
        acc = acc + _dot(y[...], w_ref[n * w:(n + 1) * w, :])
    x1_ref[...] = acc
    h_ref[...] = _rmsnorm_rows(acc, g_ref[...]).astype(h_ref.dtype)


def out_proj_call(ys, w_out, layer, x, gain, tm=512):
    m, d = x.shape
    tm = min(tm, m)
    yblk = pl.BlockSpec((tm, GROUP_WIDTH), lambda i: (i, 0))
    xblk = pl.BlockSpec((tm, d), lambda i: (i, 0))
    return pl.pallas_call(
        _out_proj_kernel,
        grid=(m // tm,),
        in_specs=[yblk] * 4 + [pl.BlockSpec((None, d, d), lambda i: (layer, 0, 0),
                                            pipeline_mode=pl.Buffered(1)), xblk,
                               pl.BlockSpec((1, d), lambda i: (0, 0))],
        out_specs=[xblk, xblk],
        out_shape=[jax.ShapeDtypeStruct((m, d), F32), jax.ShapeDtypeStruct((m, d), BF16)],
        compiler_params=_params("parallel"),
        name="out_proj",
    )(*ys, w_out, x, gain)


FFN_SUB_ROWS = 512


def _ffn_up_kernel(h_ref, wu32_ref, wg32_ref, cwu_ref, cwg_ref, cbu_ref, cbg_ref, o_ref,
                   tail_u_ref, tail_g_ref, wu_ref, wg_ref, *, tiles_per_seq):
    tm = h_ref.shape[0]

    @pl.when(pl.program_id(1) == 0)
    def _():
        wu_ref[...] = wu32_ref[...].astype(BF16)
        wg_ref[...] = wg32_ref[...].astype(BF16)

    @pl.when(pl.program_id(1) % tiles_per_seq == 0)
    def _():
        tail_u_ref[...] = jnp.zeros_like(tail_u_ref)
        tail_g_ref[...] = jnp.zeros_like(tail_g_ref)

    sub = min(FFN_SUB_ROWS, tm)
    tail_u = tail_u_ref[...]
    tail_g = tail_g_ref[...]
    for r0 in range(0, tm, sub):
        h = h_ref[r0:r0 + sub, :]
        u = _dot(h, wu_ref[...])
        g = _dot(h, wg_ref[...])
        up = _causal_conv(u, tail_u, cwu_ref[...], FFN_CONV) + cbu_ref[...]
        gate = _causal_conv(g, tail_g, cwg_ref[...], FFN_CONV) + cbg_ref[...]
        tail_u = u[sub - SUBLANE:, :]
        tail_g = g[sub - SUBLANE:, :]
        o_ref[r0:r0 + sub, :] = (_silu(gate) * up).astype(o_ref.dtype)
    tail_u_ref[...] = tail_u
    tail_g_ref[...] = tail_g


def ffn_up_call(h, w_up, layer, conv_w, conv_b, seq, tm=1024, tn=512):
    m, d = h.shape
    tm = min(tm, seq)
    nj = D_FF // tn
    return pl.pallas_call(
        functools.partial(_ffn_up_kernel, tiles_per_seq=seq // tm),
        grid=(nj, m // tm),
        in_specs=[pl.BlockSpec((tm, d), lambda j, i: (i, 0)),
                  pl.BlockSpec((None, d, tn), lambda j, i: (layer, 0, j)),
                  pl.BlockSpec((None, d, tn), lambda j, i: (layer, 0, j + nj)),
                  pl.BlockSpec((FFN_CONV, tn), lambda j, i: (0, j)),
                  pl.BlockSpec((FFN_CONV, tn), lambda j, i: (0, j + nj)),
                  pl.BlockSpec((1, tn), lambda j, i: (0, j)),
                  pl.BlockSpec((1, tn), lambda j, i: (0, j + nj))],
        out_specs=pl.BlockSpec((tm, tn), lambda j, i: (i, j)),
        out_shape=jax.ShapeDtypeStruct((m, D_FF), BF16),
        scratch_shapes=[pltpu.VMEM((SUBLANE, tn), F32), pltpu.VMEM((SUBLANE, tn), F32),
                        pltpu.VMEM((d, tn), BF16), pltpu.VMEM((d, tn), BF16)],
        compiler_params=_params("parallel", "arbitrary"),
        name="ffn_up",
    )(h, w_up, w_up, conv_w, conv_w, conv_b, conv_b)


def _ffn_down_kernel(g_ref, w_ref, x_ref, gain_ref, x2_ref, hn_ref):
    x2 = x_ref[...] + _dot(g_ref[...], w_ref[...])
    x2_ref[...] = x2
    hn_ref[...] = _rmsnorm_rows(x2, gain_ref[...]).astype(hn_ref.dtype)


def ffn_down_call(g, w_down, layer, x, gain, hn_dtype, tm=256):
    m, d = x.shape
    tm = min(tm, m)
    kdim = g.shape[1]
    xblk = pl.BlockSpec((tm, d), lambda i: (i, 0))
    return pl.pallas_call(
        _ffn_down_kernel,
        grid=(m // tm,),
        in_specs=[pl.BlockSpec((tm, kdim), lambda i: (i, 0)),
                  pl.BlockSpec((None, kdim, d), lambda i: (layer, 0, 0),
                               pipeline_mode=pl.Buffered(1)),
                  xblk, pl.BlockSpec((1, d), lambda i: (0, 0))],
        out_specs=[xblk, xblk],
        out_shape=[jax.ShapeDtypeStruct((m, d), F32), jax.ShapeDtypeStruct((m, d), hn_dtype)],
        compiler_params=_params("parallel"),
        name="ffn_down",
    )(g, w_down, x, gain)


def _block_diag(blocks):
    n, r, c = blocks.shape
    eye = jnp.eye(n, dtype=blocks.dtype)
    return (eye[:, None, :, None] * blocks[:, :, None, :]).reshape(n * r, n * c)


def _small_vec(values, offset):
    return jnp.zeros((1, LANE), F32).at[0, offset:offset + values.shape[0]].set(values)


def _w_in_segments():
    gw = GROUP_WIDTH
    sizes = (gw, gw, 3 * gw, N_HEADS, 3 * gw, gw, N_HEADS, N_HEADS, 3 * gw)
    dsts = (COL_AX, COL_AG, COL_BQKV, COL_SMALL + SMALL_F, COL_CQKV, COL_CZ,
            COL_SMALL + SMALL_BETA, COL_SMALL + SMALL_ALPHA, COL_DQKV)
    segs, src = [], 0
    for size, dst in zip(sizes, dsts):
        segs.append((src, dst, size))
        src += size
    return segs, src


def _prep_w_in_kernel(w_ref, o_ref):
    segs, _ = _w_in_segments()
    o_ref[:, COL_SMALL:] = jnp.zeros((o_ref.shape[0], Z_COLS - COL_SMALL), o_ref.dtype)
    for src, dst, size in segs:
        lo = (src // LANE) * LANE
        hi = min(-(-(src + size) // LANE) * LANE, w_ref.shape[1])
        tile = w_ref[:, lo:hi]
        o_ref[:, dst:dst + size] = tile[:, src - lo:src - lo + size].astype(o_ref.dtype)


def prep_w_in_call(w_in, tk=256):
    depth, k, n = w_in.shape
    assert n == _w_in_segments()[1]
    return pl.pallas_call(
        _prep_w_in_kernel,
        grid=(depth, k // tk),
        in_specs=[pl.BlockSpec((None, tk, n), lambda l, i: (l, i, 0))],
        out_specs=pl.BlockSpec((None, tk, Z_COLS), lambda l, i: (l, i, 0)),
        out_shape=jax.ShapeDtypeStruct((depth, k, Z_COLS), BF16),
        compiler_params=_params("parallel", "parallel"),
        name="prep_w_in",
    )(w_in)


def kernel(x, norm_mix, w_in, lru_conv_w, lru_conv_b, lru_wa, lru_ba, lru_wx, lru_bx,
           lru_lambda, fox_f_bias, gdn_conv_w, gdn_a_log, gdn_dt_bias, gdn_norm,
           norm_a, norm_b, norm_d, w_out, norm_ffn, ffn_w_up, ffn_conv_w, ffn_conv_b,
           ffn_w_down, norm_final):
    batch, seq, d = x.shape
    depth = w_in.shape[0]
    m = batch * seq
    gw = GROUP_WIDTH
    gmat64 = _block_diag(jnp.full((LRU_BLOCKS, LRU_BLOCK_DIM, LRU_BLOCK_DIM),
                                  1.0 / LRU_BLOCK_DIM, F32)).astype(BF16)
    xs = x.reshape(m, d)
    w_in_z = prep_w_in_call(w_in)
    w_out_b = w_out.astype(BF16)
    w_down_b = ffn_w_down.astype(BF16)
    h = rmsnorm_call(xs, norm_mix[0], BF16)
    for l in range(depth):
        p = {
            "lru_conv_w": lru_conv_w[l], "lru_conv_b": lru_conv_b[l].reshape(1, gw),
            "lru_wa": _block_diag(lru_wa[l]).astype(BF16), "lru_ba": lru_ba[l].reshape(1, gw),
            "lru_wx": _block_diag(lru_wx[l]).astype(BF16), "lru_bx": lru_bx[l].reshape(1, gw),
            "lru_lambda": lru_lambda[l].reshape(1, gw), "gmat64": gmat64,
            "norm_a": norm_a[l].reshape(1, gw),
            "gdn_conv_w": gdn_conv_w[l],
            "gdn_aneg": _small_vec(-jnp.exp(gdn_a_log[l]), SMALL_ALPHA),
            "gdn_dt": _small_vec(gdn_dt_bias[l], SMALL_ALPHA),
            "gdn_norm": gdn_norm[l].reshape(1, HEAD_DIM),
        }
        z = in_proj_call(h, w_in_z, l)

        y_a = lru_call(z, p, batch, seq)

        c = logf_cumsum_call(z, _small_vec(fox_f_bias[l], SMALL_F), batch, seq)
        c = c.reshape(batch, seq, LANE)[:, :, SMALL_F:SMALL_F + N_HEADS]
        c = jnp.transpose(c, (0, 2, 1))
        z3 = z.reshape(batch, seq, Z_COLS)
        y_b = fox_call(z3, c, norm_b[l].reshape(1, gw), batch, seq).reshape(m, gw)

        y_c = gdn_call(z, p, batch, seq)

        y_d = dilated_call(z3, norm_d[l].reshape(1, gw), batch, seq).reshape(m, gw)

        x1, hf = out_proj_call((y_a, y_b, y_c, y_d), w_out_b, l, xs,
                               norm_ffn[l].reshape(1, d))
        g = ffn_up_call(hf, ffn_w_up, l, ffn_conv_w[l], ffn_conv_b[l].reshape(1, 2 * D_FF), seq)
        last = l == depth - 1
        gain_next = norm_final if last else norm_mix[l + 1]
        xs, h = ffn_down_call(g, w_down_b, l, x1, gain_next.reshape(1, d),
                              F32 if last else BF16)
    return h.reshape(batch, seq, d)
```

```python
import functools
import math

import jax
import jax.numpy as jnp
from jax import lax
from jax.experimental import pallas as pl
from jax.experimental.pallas import tpu as pltpu

F32 = jnp.float32
BF16 = jnp.bfloat16

D_MODEL = 2048
GROUP_WIDTH = 512
HEAD_DIM = 128
N_HEADS = 4
LRU_BLOCKS = 8
LRU_BLOCK_DIM = 64
LRU_C = 8.0
SHORT_CONV = 4
FFN_CONV = 3
D_FF = 5632
GDN_CHUNK = 64
SPAN = 128
DILATIONS = (1, 4, 16)
EPS = 1e-6
NEG_INF = -1e30
ATTN_SCALE = HEAD_DIM ** -0.5
LOG2E = math.log2(math.e)

LANE = 128
SUBLANE = 8
VMEM_LIMIT = 52 * 1024 * 1024

COL_CQKV = 0
COL_AX = 1536
COL_AG = 2048
COL_CZ = 2560
COL_BQKV = 3072
COL_DQKV = 4608
COL_SMALL = 6144
Z_COLS = 6400
SMALL_F, SMALL_BETA, SMALL_ALPHA = 0, 4, 8


def _params(*sem):
    return pltpu.CompilerParams(dimension_semantics=sem, vmem_limit_bytes=VMEM_LIMIT)


def _sigmoid(x):
    return 1.0 / (1.0 + jnp.exp(-x))


def _softplus(x):
    return jnp.maximum(x, 0.0) + jnp.log1p(jnp.exp(-jnp.abs(x)))


def _silu(x):
    return x * _sigmoid(x)


def _gelu_tanh(x):
    c = math.sqrt(2.0 / math.pi)
    return 0.5 * x * (1.0 + jnp.tanh(c * (x + 0.044715 * (x * x * x))))


def _dot(a, b, precision=None):
    return jnp.dot(a, b, preferred_element_type=F32, precision=precision)


def _dot_nt(a, b):
    return lax.dot_general(a, b, (((1,), (1,)), ((), ())), preferred_element_type=F32)


def _dot_tn(a, b):
    return lax.dot_general(a, b, (((0,), (0,)), ((), ())), preferred_element_type=F32)


def _causal_conv(x, halo, w, taps):
    n = x.shape[0]
    cat = jnp.concatenate([halo, x], axis=0)
    y = w[taps - 1:taps, :] * x
    for k in range(taps - 1):
        shifted = pltpu.roll(cat, taps - 1 - k, 0)[SUBLANE:SUBLANE + n, :]
        y = y + w[k:k + 1, :] * shifted
    return y


def _scan_affine(a, u, h0):
    n, c = a.shape
    groups = n // SUBLANE
    a = a.reshape(groups, SUBLANE, c)
    u = u.reshape(groups, SUBLANE, c)
    pos = lax.broadcasted_iota(jnp.int32, a.shape, 1)
    s = 1
    while s < SUBLANE:
        keep = pos >= s
        a_s = jnp.where(keep, pltpu.roll(a, s, 1), 1.0)
        u_s = jnp.where(keep, pltpu.roll(u, s, 1), 0.0)
        u = a * u_s + u
        a = a * a_s
        s *= 2
    out = []
    carry = h0
    for g in range(groups):
        hg = u[g] + a[g] * carry
        carry = hg[SUBLANE - 1:SUBLANE, :]
        out.append(hg)
    return jnp.concatenate(out, axis=0)


def _scan_add(x, seg):
    row = lax.broadcasted_iota(jnp.int32, x.shape, 0)
    pos = row % seg
    s = 1
    while s < seg:
        x = x + jnp.where(pos >= s, pltpu.roll(x, s, 0), 0.0)
        s *= 2
    return x


def _rmsnorm_rows(x, gain):
    ms = jnp.mean(x * x, axis=-1, keepdims=True)
    return x * lax.rsqrt(ms + EPS) * gain


def _rmsnorm_kernel(x_ref, g_ref, o_ref):
    o_ref[...] = _rmsnorm_rows(x_ref[...], g_ref[...]).astype(o_ref.dtype)


def rmsnorm_call(x, gain, out_dtype, tm=512):
    m, d = x.shape
    return pl.pallas_call(
        _rmsnorm_kernel,
        grid=(m // tm,),
        in_specs=[pl.BlockSpec((tm, d), lambda i: (i, 0)),
                  pl.BlockSpec((1, d), lambda i: (0, 0))],
        out_specs=pl.BlockSpec((tm, d), lambda i: (i, 0)),
        out_shape=jax.ShapeDtypeStruct((m, d), out_dtype),
        compiler_params=_params("parallel"),
        name="rmsnorm",
    )(x, gain.reshape(1, d))


def _matmul_kernel(a_ref, b_ref, o_ref):
    o_ref[...] = _dot(a_ref[...], b_ref[...])


def in_proj_call(h, w, layer, tm=1024, tn=1280):
    m, k = h.shape
    n = w.shape[2]
    tm = min(tm, m)
    return pl.pallas_call(
        _matmul_kernel,
        grid=(m // tm, n // tn),
        in_specs=[pl.BlockSpec((tm, k), lambda i, j: (i, 0)),
                  pl.BlockSpec((None, k, tn), lambda i, j: (layer, 0, j))],
        out_specs=pl.BlockSpec((tm, tn), lambda i, j: (i, j)),
        out_shape=jax.ShapeDtypeStruct((m, n), F32),
        compiler_params=_params("parallel", "arbitrary"),
        name="in_proj",
    )(h, w)


def _group_mean_sq(h, gmat):
    h2 = h * h
    hi = h2.astype(BF16)
    lo = (h2 - hi.astype(F32)).astype(BF16)
    return _dot(hi, gmat) + _dot(lo, gmat)


def _lru_kernel(ax_ref, ag_ref, cw_ref, cb_ref, wa_ref, ba_ref, wx_ref, bx_ref, lam_ref,
                gmat_ref, gain_ref, o_ref, halo_ref, hprev_ref):
    tt = ax_ref.shape[0]

    @pl.when(pl.program_id(1) == 0)
    def _():
        halo_ref[...] = jnp.zeros_like(halo_ref)
        hprev_ref[...] = jnp.zeros_like(hprev_ref)

    x = ax_ref[...]
    xc = _causal_conv(x, halo_ref[...], cw_ref[...], SHORT_CONV) + cb_ref[...]
    halo_ref[...] = x[tt - SUBLANE:, :]
    xb = xc.astype(BF16)
    r = _sigmoid(_dot(xb, wa_ref[...]) + ba_ref[...])
    i = _sigmoid(_dot(xb, wx_ref[...]) + bx_ref[...])
    log_a = (-LRU_C) * r * _softplus(-lam_ref[...])
    a = jnp.exp(log_a)
    one_m_a2 = -jnp.tanh(log_a) * (a * a + 1.0)
    root = jnp.where(one_m_a2 > 0.0, one_m_a2 * lax.rsqrt(one_m_a2), 0.0)
    u = root * (i * xc)
    h = _scan_affine(a, u, hprev_ref[...])
    hprev_ref[...] = h[tt - 1:tt, :]
    y = h * lax.rsqrt(_group_mean_sq(h, gmat_ref[...]) + EPS) * gain_ref[...]
    o_ref[...] = (y * _gelu_tanh(ag_ref[...])).astype(o_ref.dtype)


def lru_call(z, p, batch, seq, tt=512):
    tt = min(tt, seq)
    nt = seq // tt
    w = GROUP_WIDTH
    row = lambda b, t: (b * nt + t)
    vec = pl.BlockSpec((1, w), lambda b, t: (0, 0))
    mat = pl.BlockSpec((w, w), lambda b, t: (0, 0))
    return pl.pallas_call(
        _lru_kernel,
        grid=(batch, nt),
        in_specs=[pl.BlockSpec((tt, w), lambda b, t: (row(b, t), COL_AX // w)),
                  pl.BlockSpec((tt, w), lambda b, t: (row(b, t), COL_AG // w)),
                  pl.BlockSpec((SHORT_CONV, w), lambda b, t: (0, 0)),
                  vec, mat, vec, mat, vec, vec, mat, vec],
        out_specs=pl.BlockSpec((tt, w), lambda b, t: (row(b, t), 0)),
        out_shape=jax.ShapeDtypeStruct((batch * seq, w), BF16),
        scratch_shapes=[pltpu.VMEM((SUBLANE, w), F32), pltpu.VMEM((1, w), F32)],
        compiler_params=_params("parallel", "arbitrary"),
        name="rg_lru",
    )(z, z, p["lru_conv_w"], p["lru_conv_b"], p["lru_wa"], p["lru_ba"], p["lru_wx"],
      p["lru_bx"], p["lru_lambda"], p["gmat64"], p["norm_a"])


def _logf_cumsum_kernel(s_ref, bias_ref, o_ref, carry_ref):
    tt = s_ref.shape[0]

    @pl.when(pl.program_id(1) == 0)
    def _():
        carry_ref[...] = jnp.zeros_like(carry_ref)

    log_f = -_softplus(-(s_ref[...] + bias_ref[...]))
    c = _scan_add(log_f, tt) + carry_ref[...]
    carry_ref[...] = c[tt - 1:tt, :]
    o_ref[...] = c


def logf_cumsum_call(z, bias, batch, seq, tt=1024):
    tt = min(tt, seq)
    nt = seq // tt
    return pl.pallas_call(
        _logf_cumsum_kernel,
        grid=(batch, nt),
        in_specs=[pl.BlockSpec((tt, LANE), lambda b, t: (b * nt + t, COL_SMALL // LANE)),
                  pl.BlockSpec((1, LANE), lambda b, t: (0, 0))],
        out_specs=pl.BlockSpec((tt, LANE), lambda b, t: (b * nt + t, 0)),
        out_shape=jax.ShapeDtypeStruct((batch * seq, LANE), F32),
        scratch_shapes=[pltpu.VMEM((1, LANE), F32)],
        compiler_params=_params("parallel", "arbitrary"),
        name="logf_cumsum",
    )(z, bias)


FOX_ROW_CHUNK = 64


def _fox_kernel(q_ref, k_ref, v_ref, cq_ref, ck_ref, gain_ref, o_ref, kb_ref, vb_ref,
                s_s, p_s, m_s, alpha_s, cq_s, acc_s, *, tq, tk):
    qi = pl.program_id(2)

    @pl.when(qi == 0)
    def _():
        kb_ref[...] = k_ref[...].astype(BF16)
        vb_ref[:, :HEAD_DIM] = v_ref[...].astype(BF16)
        vb_ref[:, HEAD_DIM:] = jnp.ones((vb_ref.shape[0], HEAD_DIM), BF16)

    q = (q_ref[...] * (ATTN_SCALE * LOG2E)).astype(BF16)
    cq_s[...] = jnp.broadcast_to(cq_ref[...] * LOG2E, (tq, LANE))
    m_s[...] = jnp.full((tq, LANE), NEG_INF, F32)
    acc_s[...] = jnp.zeros((tq, 2 * HEAD_DIM), F32)
    rc = min(FOX_ROW_CHUNK, tq)

    def lanes(x, n):
        return jnp.concatenate([x] * (n // LANE), axis=1)

    def scores(kv, slot):
        start = pl.multiple_of(kv * tk, tk)
        s_s[slot] = _dot_nt(q, kb_ref[pl.ds(start, tk), :]) - ck_ref[kv] * LOG2E

    def update(kv, slot, diagonal):
        for r0 in range(0, tq, rc):
            rows = slice(r0, r0 + rc)
            s = s_s[slot, rows, :]
            if diagonal:
                row = lax.broadcasted_iota(jnp.int32, s.shape, 0) + r0
                col = lax.broadcasted_iota(jnp.int32, s.shape, 1)
                s = jnp.where(col <= row, s, NEG_INF)
            m_old = m_s[rows, :]
            m_new = jnp.maximum(m_old, jnp.max(s, axis=-1, keepdims=True) + cq_s[rows, :])
            p_s[rows, :] = jnp.exp2(s - lanes(m_new - cq_s[rows, :], tk)).astype(BF16)
            alpha_s[rows, :] = jnp.exp2(m_old - m_new)
            m_s[rows, :] = m_new
        start = pl.multiple_of(kv * tk, tk)
        pv = _dot(p_s[...], vb_ref[pl.ds(start, tk), :])
        acc_s[...] = lanes(alpha_s[...], 2 * HEAD_DIM) * acc_s[...] + pv

    def pair(jj, carry):
        scores(2 * jj + 1, 1)
        update(2 * jj, 0, False)
        scores(2 * jj + 2, 0)
        update(2 * jj + 1, 1, False)
        return carry

    scores(0, 0)
    lax.fori_loop(0, qi // 2, pair, 0)

    @pl.when(qi % 2 == 1)
    def _():
        scores(qi, 1)
        update(qi - 1, 0, False)
        update(qi, 1, True)

    @pl.when(qi % 2 == 0)
    def _():
        update(qi, 0, True)

    acc = acc_s[...]
    o = acc[:, :HEAD_DIM] / acc[:, HEAD_DIM:]
    o = o * lax.rsqrt(jnp.mean(o * o, axis=-1, keepdims=True) + EPS) * gain_ref[...]
    o_ref[...] = o.astype(o_ref.dtype)


def fox_call(z3, c, gain, batch, seq, tq=512):
    tq = min(tq, seq)
    tk = tq
    nq = seq // tq
    nk = seq // tk
    qb, kb, vb = (COL_BQKV // LANE, (COL_BQKV + GROUP_WIDTH) // LANE,
                  (COL_BQKV + 2 * GROUP_WIDTH) // LANE)
    c_col = c.reshape(batch, N_HEADS, seq, 1)
    c_row = c.reshape(batch, N_HEADS, nk, 1, tk)
    return pl.pallas_call(
        functools.partial(_fox_kernel, tq=tq, tk=tk),
        grid=(batch, N_HEADS, nq),
        in_specs=[pl.BlockSpec((None, tq, LANE), lambda b, h, i: (b, i, qb + h)),
                  pl.BlockSpec((None, seq, LANE), lambda b, h, i: (b, 0, kb + h)),
                  pl.BlockSpec((None, seq, LANE), lambda b, h, i: (b, 0, vb + h)),
                  pl.BlockSpec((None, None, tq, 1), lambda b, h, i: (b, h, i, 0)),
                  pl.BlockSpec((None, None, nk, 1, tk), lambda b, h, i: (b, h, 0, 0, 0)),
                  pl.BlockSpec((1, LANE), lambda b, h, i: (0, h))],
        out_specs=pl.BlockSpec((None, tq, LANE), lambda b, h, i: (b, i, h)),
        out_shape=jax.ShapeDtypeStruct((batch, seq, GROUP_WIDTH), BF16),
        scratch_shapes=[pltpu.VMEM((seq, LANE), BF16), pltpu.VMEM((seq, 2 * LANE), BF16),
                        pltpu.VMEM((2, tq, tk), F32), pltpu.VMEM((tq, tk), BF16),
                        pltpu.VMEM((tq, LANE), F32), pltpu.VMEM((tq, LANE), F32),
                        pltpu.VMEM((tq, LANE), F32), pltpu.VMEM((tq, 2 * HEAD_DIM), F32)],
        compiler_params=_params("parallel", "parallel", "arbitrary"),
        name="fox_attention",
    )(z3, z3, z3, c_col, c_row, gain)


def _l2norm(t):
    return t * lax.rsqrt(jnp.sum(t * t, axis=-1, keepdims=True) + EPS)


def _bmm(a, b):
    return lax.dot_general(a, b, (((2,), (1,)), ((0,), (0,))), preferred_element_type=F32)


def _bmm_nt(a, b):
    return lax.dot_general(a, b, (((2,), (2,)), ((0,), (0,))), preferred_element_type=F32)


def _bmm_tn(a, b):
    return lax.dot_general(a, b, (((1,), (1,)), ((0,), (0,))), preferred_element_type=F32)


def _neumann_tail(p):
    c = p.shape[-1]
    a = p
    q = p
    cover = 2
    while cover < c:
        qb = q.astype(BF16)
        q = _bmm(qb, qb)
        a = a + q + _bmm(a.astype(BF16), q.astype(BF16))
        cover *= 2
    return a


def _gdn_kernel(x_ref, zg_ref, s_ref, cw_ref, aneg_ref, dt_ref, gain_ref, o_ref,
                halo_ref, qkv_ref, state_ref):
    tt = x_ref.shape[0]
    c = GDN_CHUNK
    nc = tt // c
    w = GROUP_WIDTH
    hd = HEAD_DIM

    @pl.when(pl.program_id(1) == 0)
    def _():
        halo_ref[...] = jnp.zeros_like(halo_ref)
        state_ref[...] = jnp.zeros_like(state_ref)

    x = x_ref[...]
    qkv_ref[...] = _silu(_causal_conv(x, halo_ref[...], cw_ref[...], SHORT_CONV))
    halo_ref[...] = x[tt - SUBLANE:, :]
    small = s_ref[...]
    beta_all = _sigmoid(small)
    gc_all = _scan_add(aneg_ref[...] * _softplus(small + dt_ref[...]), c)

    pairs = [(ci, h) for ci in range(nc) for h in range(N_HEADS)]

    def gather(col0):
        return jnp.stack([qkv_ref[ci * c:(ci + 1) * c, col0 + h * hd:col0 + (h + 1) * hd]
                          for ci, h in pairs])

    q = _l2norm(gather(0)) * ATTN_SCALE
    k = _l2norm(gather(w))
    v = gather(2 * w)
    beta = jnp.stack([beta_all[ci * c:(ci + 1) * c, SMALL_BETA + h:SMALL_BETA + h + 1]
                      for ci, h in pairs])
    gc = jnp.stack([gc_all[ci * c:(ci + 1) * c, SMALL_ALPHA + h:SMALL_ALPHA + h + 1]
                    for ci, h in pairs])
    gc_t = [gc_all[ci * c:(ci + 1) * c, :].T for ci in range(nc)]
    gc_row = jnp.stack([gc_t[ci][SMALL_ALPHA + h:SMALL_ALPHA + h + 1, :]
                        for ci, h in pairs])

    row = lax.broadcasted_iota(jnp.int32, (c, c), 0)
    col = lax.broadcasted_iota(jnp.int32, (c, c), 1)
    tril = col <= row
    strict = col < row
    decay = jnp.where(tril, jnp.exp(jnp.where(tril, gc - gc_row, 0.0)), 0.0)
    kb = k.astype(BF16)
    kbeta = k * beta
    kk = _bmm_nt(kbeta.astype(BF16), kb) * decay
    tail = _neumann_tail(jnp.where(strict, -kk, 0.0))
    rhs = jnp.concatenate([v * beta, kbeta * jnp.exp(gc)], axis=-1)
    sol = rhs + _bmm(tail.astype(BF16), rhs.astype(BF16))
    u = sol[:, :, :hd]
    wy = sol[:, :, hd:]
    qk = jnp.where(tril, _bmm_nt(q.astype(BF16), kb) * decay, 0.0).astype(BF16)
    g_last = gc[:, c - 1:c, :]
    k_dec = (k * jnp.exp(g_last - gc)).astype(BF16)
    e_last = jnp.exp(g_last)
    wq = jnp.concatenate([wy.astype(BF16), (q * jnp.exp(gc)).astype(BF16)], axis=1)

    state = state_ref[...]
    for ci in range(nc):
        sl = slice(ci * N_HEADS, (ci + 1) * N_HEADS)
        ws = _bmm(wq[sl], state.astype(BF16))
        v_new = (u[sl] - ws[:, :c, :]).astype(BF16)
        o = ws[:, c:, :] + _bmm(qk[sl], v_new)
        state = state * e_last[sl] + _bmm_tn(k_dec[sl], v_new)
        o = o * lax.rsqrt(jnp.mean(o * o, axis=-1, keepdims=True) + EPS) * gain_ref[...]
        for h in range(N_HEADS):
            zg = zg_ref[ci * c:(ci + 1) * c, h * hd:(h + 1) * hd]
            o_ref[ci * c:(ci + 1) * c, h * hd:(h + 1) * hd] = (o[h] * _silu(zg)).astype(o_ref.dtype)
    state_ref[...] = state


def gdn_call(z, p, batch, seq, tt=512):
    tt = min(tt, seq)
    nt = seq // tt
    w = GROUP_WIDTH
    vec = pl.BlockSpec((1, LANE), lambda b, t: (0, 0))
    return pl.pallas_call(
        _gdn_kernel,
        grid=(batch, nt),
        in_specs=[pl.BlockSpec((tt, 3 * w), lambda b, t: (b * nt + t, COL_CQKV // (3 * w))),
                  pl.BlockSpec((tt, w), lambda b, t: (b * nt + t, COL_CZ // w)),
                  pl.BlockSpec((tt, LANE), lambda b, t: (b * nt + t, COL_SMALL // LANE)),
                  pl.BlockSpec((SHORT_CONV, 3 * w), lambda b, t: (0, 0)),
                  vec, vec, vec],
        out_specs=pl.BlockSpec((tt, w), lambda b, t: (b * nt + t, 0)),
        out_shape=jax.ShapeDtypeStruct((batch * seq, w), BF16),
        scratch_shapes=[pltpu.VMEM((SUBLANE, 3 * w), F32),
                        pltpu.VMEM((tt, 3 * w), F32),
                        pltpu.VMEM((N_HEADS, HEAD_DIM, HEAD_DIM), F32)],
        compiler_params=_params("parallel", "arbitrary"),
        name="gated_deltanet",
    )(z, z, z, p["gdn_conv_w"], p["gdn_aneg"], p["gdn_dt"], p["gdn_norm"])


def _dilated_kernel(q_ref, kc_ref, kp_ref, vc_ref, vp_ref, gain_ref, o_ref,
                    m_ref, acc_ref, l_ref):
    tt = q_ref.shape[0]
    first = pl.program_id(2) == 0
    i = lax.broadcasted_iota(jnp.int32, (SPAN, 2 * SPAN), 0)
    j = lax.broadcasted_iota(jnp.int32, (SPAN, 2 * SPAN), 1)
    band = jnp.logical_and(j >= i, j <= i + SPAN)
    band_first = jnp.logical_and(band, jnp.logical_or(j >= SPAN, jnp.logical_not(first)))
    ones = jnp.ones((2 * SPAN, HEAD_DIM), BF16)

    def rows(ref, start, n, dil):
        return ref[pl.ds(start, n, stride=dil), :] if dil > 1 else ref[pl.ds(start, n), :]

    for g, dil in enumerate(DILATIONS):
        for r in range(dil):
            for sb in range(tt // (dil * SPAN)):
                q0 = r + dil * SPAN * sb
                q = (rows(q_ref, q0, SPAN, dil) * (ATTN_SCALE * LOG2E)).astype(BF16)
                if sb == 0:
                    p0 = tt - dil * SPAN + r
                    k = jnp.concatenate([rows(kp_ref, p0, SPAN, dil), rows(kc_ref, r, SPAN, dil)], 0)
                    v = jnp.concatenate([rows(vp_ref, p0, SPAN, dil), rows(vc_ref, r, SPAN, dil)], 0)
                else:
                    k = rows(kc_ref, q0 - dil * SPAN, 2 * SPAN, dil)
                    v = rows(vc_ref, q0 - dil * SPAN, 2 * SPAN, dil)
                s = _dot_nt(q, k.astype(BF16))
                s = jnp.where(band_first if sb == 0 else band, s, NEG_INF)
                v1 = jnp.concatenate([v.astype(BF16), ones], axis=1)
                m_blk = jnp.broadcast_to(jnp.max(s, axis=-1, keepdims=True), (SPAN, LANE))
                if g == 0:
                    m_new = m_blk
                else:
                    m_old = rows(m_ref, q0, SPAN, dil)
                    m_new = jnp.maximum(m_old, m_blk)
                p = jnp.exp2(s - jnp.concatenate([m_new, m_new], axis=1))
                pv = _dot(p.astype(BF16), v1)
                acc, l = pv[:, :HEAD_DIM], pv[:, HEAD_DIM:]
                if g > 0:
                    a = jnp.exp2(m_old - m_new)
                    acc = acc + a * rows(acc_ref, q0, SPAN, dil)
                    l = l + a * rows(l_ref, q0, SPAN, dil)
                dst = pl.ds(q0, SPAN, stride=dil) if dil > 1 else pl.ds(q0, SPAN)
                m_ref[dst, :] = m_new
                acc_ref[dst, :] = acc
                l_ref[dst, :] = l
    y = acc_ref[...] / l_ref[...]
    y = y * lax.rsqrt(jnp.mean(y * y, axis=-1, keepdims=True) + EPS) * gain_ref[...]
    o_ref[...] = y.astype(o_ref.dtype)


def dilated_call(z3, gain, batch, seq):
    tt = max(DILATIONS) * SPAN
    assert seq % tt == 0
    nt = seq // tt
    qb, kb, vb = (COL_DQKV // LANE, (COL_DQKV + GROUP_WIDTH) // LANE,
                  (COL_DQKV + 2 * GROUP_WIDTH) // LANE)

    def cur(off):
        return pl.BlockSpec((None, tt, LANE), lambda b, h, i: (b, i, off + h))

    def prev(off):
        return pl.BlockSpec((None, tt, LANE), lambda b, h, i: (b, jnp.maximum(i - 1, 0), off + h))

    return pl.pallas_call(
        _dilated_kernel,
        grid=(batch, N_HEADS, nt),
        in_specs=[cur(qb), cur(kb), prev(kb), cur(vb), prev(vb),
                  pl.BlockSpec((1, LANE), lambda b, h, i: (0, h))],
        out_specs=pl.BlockSpec((None, tt, LANE), lambda b, h, i: (b, i, h)),
        out_shape=jax.ShapeDtypeStruct((batch, seq, GROUP_WIDTH), BF16),
        scratch_shapes=[pltpu.VMEM((tt, LANE), F32), pltpu.VMEM((tt, HEAD_DIM), F32),
                        pltpu.VMEM((tt, LANE), F32)],
        compiler_params=_params("parallel", "parallel", "arbitrary"),
        name="dilated_attention",
    )(z3, z3, z3, z3, z3, gain)


def _out_proj_kernel(ya, yb, yc, yd, w_ref, x_ref, g_ref, x1_ref, h_ref):
    w = GROUP_WIDTH
    acc = x_ref[...]
    for n, y in enumerate((ya, yb, yc, yd)):
        acc = acc + _dot(y[...], w_ref[n * w:(n + 1) * w, :])
    x1_ref[...] = acc
    h_ref[...] = _rmsnorm_rows(acc, g_ref[...]).astype(h_ref.dtype)


def out_proj_call(ys, w_out, layer, x, gain, tm=512):
    m, d = x.shape
    tm = min(tm, m)
    yblk = pl.BlockSpec((tm, GROUP_WIDTH), lambda i: (i, 0))
    xblk = pl.BlockSpec((tm, d), lambda i: (i, 0))
    return pl.pallas_call(
        _out_proj_kernel,
        grid=(m // tm,),
        in_specs=[yblk] * 4 + [pl.BlockSpec((None, d, d), lambda i: (layer, 0, 0),
                                            pipeline_mode=pl.Buffered(1)), xblk,
                               pl.BlockSpec((1, d), lambda i: (0, 0))],
        out_specs=[xblk, xblk],
        out_shape=[jax.ShapeDtypeStruct((m, d), F32), jax.ShapeDtypeStruct((m, d), BF16)],
        compiler_params=_params("parallel"),
        name="out_proj",
    )(*ys, w_out, x, gain)


FFN_SUB_ROWS = 512


def _ffn_up_kernel(h_ref, wu32_ref, wg32_ref, cwu_ref, cwg_ref, cbu_ref, cbg_ref, o_ref,
                   tail_u_ref, tail_g_ref, wu_ref, wg_ref, *, tiles_per_seq):
    tm = h_ref.shape[0]

    @pl.when(pl.program_id(1) == 0)
    def _():
        wu_ref[...] = wu32_ref[...].astype(BF16)
        wg_ref[...] = wg32_ref[...].astype(BF16)

    @pl.when(pl.program_id(1) % tiles_per_seq == 0)
    def _():
        tail_u_ref[...] = jnp.zeros_like(tail_u_ref)
        tail_g_ref[...] = jnp.zeros_like(tail_g_ref)

    sub = min(FFN_SUB_ROWS, tm)
    tail_u = tail_u_ref[...]
    tail_g = tail_g_ref[...]
    for r0 in range(0, tm, sub):
        h = h_ref[r0:r0 + sub, :]
        u = _dot(h, wu_ref[...])
        g = _dot(h, wg_ref[...])
        up = _causal_conv(u, tail_u, cwu_ref[...], FFN_CONV) + cbu_ref[...]
        gate = _causal_conv(g, tail_g, cwg_ref[...], FFN_CONV) + cbg_ref[...]
        tail_u = u[sub - SUBLANE:, :]
        tail_g = g[sub - SUBLANE:, :]
        o_ref[r0:r0 + sub, :] = (_silu(gate) * up).astype(o_ref.dtype)
    tail_u_ref[...] = tail_u
    tail_g_ref[...] = tail_g


def ffn_up_call(h, w_up, layer, conv_w, conv_b, seq, tm=2048, tn=512):
    m, d = h.shape
    tm = min(tm, seq)
    nj = D_FF // tn
    return pl.pallas_call(
        functools.partial(_ffn_up_kernel, tiles_per_seq=seq // tm),
        grid=(nj, m // tm),
        in_specs=[pl.BlockSpec((tm, d), lambda j, i: (i, 0)),
                  pl.BlockSpec((None, d, tn), lambda j, i: (layer, 0, j)),
                  pl.BlockSpec((None, d, tn), lambda j, i: (layer, 0, j + nj)),
                  pl.BlockSpec((FFN_CONV, tn), lambda j, i: (0, j)),
                  pl.BlockSpec((FFN_CONV, tn), lambda j, i: (0, j + nj)),
                  pl.BlockSpec((1, tn), lambda j, i: (0, j)),
                  pl.BlockSpec((1, tn), lambda j, i: (0, j + nj))],
        out_specs=pl.BlockSpec((tm, tn), lambda j, i: (i, j)),
        out_shape=jax.ShapeDtypeStruct((m, D_FF), BF16),
        scratch_shapes=[pltpu.VMEM((SUBLANE, tn), F32), pltpu.VMEM((SUBLANE, tn), F32),
                        pltpu.VMEM((d, tn), BF16), pltpu.VMEM((d, tn), BF16)],
        compiler_params=_params("parallel", "arbitrary"),
        name="ffn_up",
    )(h, w_up, w_up, conv_w, conv_w, conv_b, conv_b)


def _ffn_down_kernel(g_ref, w_ref, x_ref, gain_ref, x2_ref, hn_ref):
    x2 = x_ref[...] + _dot(g_ref[...], w_ref[...])
    x2_ref[...] = x2
    hn_ref[...] = _rmsnorm_rows(x2, gain_ref[...]).astype(hn_ref.dtype)


def ffn_down_call(g, w_down, layer, x, gain, hn_dtype, tm=256):
    m, d = x.shape
    tm = min(tm, m)
    kdim = g.shape[1]
    xblk = pl.BlockSpec((tm, d), lambda i: (i, 0))
    return pl.pallas_call(
        _ffn_down_kernel,
        grid=(m // tm,),
        in_specs=[pl.BlockSpec((tm, kdim), lambda i: (i, 0)),
                  pl.BlockSpec((None, kdim, d), lambda i: (layer, 0, 0),
                               pipeline_mode=pl.Buffered(1)),
                  xblk, pl.BlockSpec((1, d), lambda i: (0, 0))],
        out_specs=[xblk, xblk],
        out_shape=[jax.ShapeDtypeStruct((m, d), F32), jax.ShapeDtypeStruct((m, d), hn_dtype)],
        compiler_params=_params("parallel"),
        name="ffn_down",
    )(g, w_down, x, gain)


def _block_diag(blocks):
    n, r, c = blocks.shape
    eye = jnp.eye(n, dtype=blocks.dtype)
    return (eye[:, None, :, None] * blocks[:, :, None, :]).reshape(n * r, n * c)


def _small_vec(values, offset):
    return jnp.zeros((1, LANE), F32).at[0, offset:offset + values.shape[0]].set(values)


def _w_in_segments():
    gw = GROUP_WIDTH
    sizes = (gw, gw, 3 * gw, N_HEADS, 3 * gw, gw, N_HEADS, N_HEADS, 3 * gw)
    dsts = (COL_AX, COL_AG, COL_BQKV, COL_SMALL + SMALL_F, COL_CQKV, COL_CZ,
            COL_SMALL + SMALL_BETA, COL_SMALL + SMALL_ALPHA, COL_DQKV)
    segs, src = [], 0
    for size, dst in zip(sizes, dsts):
        segs.append((src, dst, size))
        src += size
    return segs, src


def _prep_w_in_kernel(w_ref, o_ref):
    segs, _ = _w_in_segments()
    o_ref[:, COL_SMALL:] = jnp.zeros((o_ref.shape[0], Z_COLS - COL_SMALL), o_ref.dtype)
    for src, dst, size in segs:
        lo = (src // LANE) * LANE
        hi = min(-(-(src + size) // LANE) * LANE, w_ref.shape[1])
        tile = w_ref[:, lo:hi]
        o_ref[:, dst:dst + size] = tile[:, src - lo:src - lo + size].astype(o_ref.dtype)


def prep_w_in_call(w_in, tk=256):
    depth, k, n = w_in.shape
    assert n == _w_in_segments()[1]
    return pl.pallas_call(
        _prep_w_in_kernel,
        grid=(depth, k // tk),
        in_specs=[pl.BlockSpec((None, tk, n), lambda l, i: (l, i, 0))],
        out_specs=pl.BlockSpec((None, tk, Z_COLS), lambda l, i: (l, i, 0)),
        out_shape=jax.ShapeDtypeStruct((depth, k, Z_COLS), BF16),
        compiler_params=_params("parallel", "parallel"),
        name="prep_w_in",
    )(w_in)


def kernel(x, norm_mix, w_in, lru_conv_w, lru_conv_b, lru_wa, lru_ba, lru_wx, lru_bx,
           lru_lambda, fox_f_bias, gdn_conv_w, gdn_a_log, gdn_dt_bias, gdn_norm,
           norm_a, norm_b, norm_d, w_out, norm_ffn, ffn_w_up, ffn_conv_w, ffn_conv_b,
           ffn_w_down, norm_final):
    batch, seq, d = x.shape
    depth = w_in.shape[0]
    m = batch * seq
    gw = GROUP_WIDTH
    gmat64 = _block_diag(jnp.full((LRU_BLOCKS, LRU_BLOCK_DIM, LRU_BLOCK_DIM),
                                  1.0 / LRU_BLOCK_DIM, F32)).astype(BF16)
    xs = x.reshape(m, d)
    w_in_z = prep_w_in_call(w_in)
    w_out_b = w_out.astype(BF16)
    w_down_b = ffn_w_down.astype(BF16)
    h = rmsnorm_call(xs, norm_mix[0], BF16)
    for l in range(depth):
        p = {
            "lru_conv_w": lru_conv_w[l], "lru_conv_b": lru_conv_b[l].reshape(1, gw),
            "lru_wa": _block_diag(lru_wa[l]).astype(BF16), "lru_ba": lru_ba[l].reshape(1, gw),
            "lru_wx": _block_diag(lru_wx[l]).astype(BF16), "lru_bx": lru_bx[l].reshape(1, gw),
            "lru_lambda": lru_lambda[l].reshape(1, gw), "gmat64": gmat64,
            "norm_a": norm_a[l].reshape(1, gw),
            "gdn_conv_w": gdn_conv_w[l],
            "gdn_aneg": _small_vec(-jnp.exp(gdn_a_log[l]), SMALL_ALPHA),
            "gdn_dt": _small_vec(gdn_dt_bias[l], SMALL_ALPHA),
            "gdn_norm": gdn_norm[l].reshape(1, HEAD_DIM),
        }
        z = in_proj_call(h, w_in_z, l)

        y_a = lru_call(z, p, batch, seq)

        c = logf_cumsum_call(z, _small_vec(fox_f_bias[l], SMALL_F), batch, seq)
        c = c.reshape(batch, seq, LANE)[:, :, SMALL_F:SMALL_F + N_HEADS]
        c = jnp.transpose(c, (0, 2, 1))
        z3 = z.reshape(batch, seq, Z_COLS)
        y_b = fox_call(z3, c, norm_b[l].reshape(1, gw), batch, seq).reshape(m, gw)

        y_c = gdn_call(z, p, batch, seq)

        y_d = dilated_call(z3, norm_d[l].reshape(1, gw), batch, seq).reshape(m, gw)

        x1, hf = out_proj_call((y_a, y_b, y_c, y_d), w_out_b, l, xs,
                               norm_ffn[l].reshape(1, d))
        g = ffn_up_call(hf, ffn_w_up, l, ffn_conv_w[l], ffn_conv_b[l].reshape(1, 2 * D_FF), seq)
        last = l == depth - 1
        gain_next = norm_final if last else norm_mix[l + 1]
        xs, h = ffn_down_call(g, w_down_b, l, x1, gain_next.reshape(1, d),
                              F32 if last else BF16)
    return h.reshape(batch, seq, d)
```

```python
import functools
import math

import jax
import jax.numpy as jnp
from jax import lax
from jax.experimental import pallas as pl
from jax.experimental.pallas import tpu as pltpu

F32 = jnp.float32
BF16 = jnp.bfloat16

D_MODEL = 2048
GROUP_WIDTH = 512
HEAD_DIM = 128
N_HEADS = 4
LRU_BLOCKS = 8
LRU_BLOCK_DIM = 64
LRU_C = 8.0
SHORT_CONV = 4
FFN_CONV = 3
D_FF = 5632
GDN_CHUNK = 64
SPAN = 128
DILATIONS = (1, 4, 16)
EPS = 1e-6
NEG_INF = -1e30
ATTN_SCALE = HEAD_DIM ** -0.5
LOG2E = math.log2(math.e)

LANE = 128
SUBLANE = 8
VMEM_LIMIT = 52 * 1024 * 1024

COL_CQKV = 0
COL_AX = 1536
COL_AG = 2048
COL_CZ = 2560
COL_BQKV = 3072
COL_DQKV = 4608
COL_SMALL = 6144
Z_COLS = 6400
SMALL_F, SMALL_BETA, SMALL_ALPHA = 0, 4, 8


def _params(*sem):
    return pltpu.CompilerParams(dimension_semantics=sem, vmem_limit_bytes=VMEM_LIMIT)


def _sigmoid(x):
    return 1.0 / (1.0 + jnp.exp(-x))


def _softplus(x):
    return jnp.maximum(x, 0.0) + jnp.log1p(jnp.exp(-jnp.abs(x)))


def _silu(x):
    return x * _sigmoid(x)


def _gelu_tanh(x):
    c = math.sqrt(2.0 / math.pi)
    return 0.5 * x * (1.0 + jnp.tanh(c * (x + 0.044715 * (x * x * x))))


def _dot(a, b, precision=None):
    return jnp.dot(a, b, preferred_element_type=F32, precision=precision)


def _dot_nt(a, b):
    return lax.dot_general(a, b, (((1,), (1,)), ((), ())), preferred_element_type=F32)


def _dot_tn(a, b):
    return lax.dot_general(a, b, (((0,), (0,)), ((), ())), preferred_element_type=F32)


def _causal_conv(x, halo, w, taps):
    n = x.shape[0]
    cat = jnp.concatenate([halo, x], axis=0)
    y = w[taps - 1:taps, :] * x
    for k in range(taps - 1):
        shifted = pltpu.roll(cat, taps - 1 - k, 0)[SUBLANE:SUBLANE + n, :]
        y = y + w[k:k + 1, :] * shifted
    return y


def _scan_affine(a, u, h0):
    n, c = a.shape
    groups = n // SUBLANE
    a = a.reshape(groups, SUBLANE, c)
    u = u.reshape(groups, SUBLANE, c)
    pos = lax.broadcasted_iota(jnp.int32, a.shape, 1)
    s = 1
    while s < SUBLANE:
        keep = pos >= s
        a_s = jnp.where(keep, pltpu.roll(a, s, 1), 1.0)
        u_s = jnp.where(keep, pltpu.roll(u, s, 1), 0.0)
        u = a * u_s + u
        a = a * a_s
        s *= 2
    out = []
    carry = h0
    for g in range(groups):
        hg = u[g] + a[g] * carry
        carry = hg[SUBLANE - 1:SUBLANE, :]
        out.append(hg)
    return jnp.concatenate(out, axis=0)


def _scan_add(x, seg):
    row = lax.broadcasted_iota(jnp.int32, x.shape, 0)
    pos = row % seg
    s = 1
    while s < seg:
        x = x + jnp.where(pos >= s, pltpu.roll(x, s, 0), 0.0)
        s *= 2
    return x


def _rmsnorm_rows(x, gain):
    ms = jnp.mean(x * x, axis=-1, keepdims=True)
    return x * lax.rsqrt(ms + EPS) * gain


def _rmsnorm_kernel(x_ref, g_ref, o_ref):
    o_ref[...] = _rmsnorm_rows(x_ref[...], g_ref[...]).astype(o_ref.dtype)


def rmsnorm_call(x, gain, out_dtype, tm=512):
    m, d = x.shape
    return pl.pallas_call(
        _rmsnorm_kernel,
        grid=(m // tm,),
        in_specs=[pl.BlockSpec((tm, d), lambda i: (i, 0)),
                  pl.BlockSpec((1, d), lambda i: (0, 0))],
        out_specs=pl.BlockSpec((tm, d), lambda i: (i, 0)),
        out_shape=jax.ShapeDtypeStruct((m, d), out_dtype),
        compiler_params=_params("parallel"),
        name="rmsnorm",
    )(x, gain.reshape(1, d))


def _matmul_kernel(a_ref, b_ref, o_ref):
    o_ref[...] = _dot(a_ref[...], b_ref[...])


def in_proj_call(h, w, layer, tm=1024, tn=1280):
    m, k = h.shape
    n = w.shape[2]
    tm = min(tm, m)
    return pl.pallas_call(
        _matmul_kernel,
        grid=(m // tm, n // tn),
        in_specs=[pl.BlockSpec((tm, k), lambda i, j: (i, 0)),
                  pl.BlockSpec((None, k, tn), lambda i, j: (layer, 0, j))],
        out_specs=pl.BlockSpec((tm, tn), lambda i, j: (i, j)),
        out_shape=jax.ShapeDtypeStruct((m, n), F32),
        compiler_params=_params("parallel", "arbitrary"),
        name="in_proj",
    )(h, w)


def _group_mean_sq(h, gmat):
    h2 = h * h
    hi = h2.astype(BF16)
    lo = (h2 - hi.astype(F32)).astype(BF16)
    return _dot(hi, gmat) + _dot(lo, gmat)


def _lru_kernel(ax_ref, ag_ref, cw_ref, cb_ref, wa_ref, ba_ref, wx_ref, bx_ref, lam_ref,
                gmat_ref, gain_ref, o_ref, halo_ref, hprev_ref):
    tt = ax_ref.shape[0]

    @pl.when(pl.program_id(1) == 0)
    def _():
        halo_ref[...] = jnp.zeros_like(halo_ref)
        hprev_ref[...] = jnp.zeros_like(hprev_ref)

    x = ax_ref[...]
    xc = _causal_conv(x, halo_ref[...], cw_ref[...], SHORT_CONV) + cb_ref[...]
    halo_ref[...] = x[tt - SUBLANE:, :]
    xb = xc.astype(BF16)
    r = _sigmoid(_dot(xb, wa_ref[...]) + ba_ref[...])
    i = _sigmoid(_dot(xb, wx_ref[...]) + bx_ref[...])
    log_a = (-LRU_C) * r * _softplus(-lam_ref[...])
    a = jnp.exp(log_a)
    one_m_a2 = -jnp.tanh(log_a) * (a * a + 1.0)
    root = jnp.where(one_m_a2 > 0.0, one_m_a2 * lax.rsqrt(one_m_a2), 0.0)
    u = root * (i * xc)
    h = _scan_affine(a, u, hprev_ref[...])
    hprev_ref[...] = h[tt - 1:tt, :]
    y = h * lax.rsqrt(_group_mean_sq(h, gmat_ref[...]) + EPS) * gain_ref[...]
    o_ref[...] = (y * _gelu_tanh(ag_ref[...])).astype(o_ref.dtype)


def lru_call(z, p, batch, seq, tt=512):
    tt = min(tt, seq)
    nt = seq // tt
    w = GROUP_WIDTH
    row = lambda b, t: (b * nt + t)
    vec = pl.BlockSpec((1, w), lambda b, t: (0, 0))
    mat = pl.BlockSpec((w, w), lambda b, t: (0, 0))
    return pl.pallas_call(
        _lru_kernel,
        grid=(batch, nt),
        in_specs=[pl.BlockSpec((tt, w), lambda b, t: (row(b, t), COL_AX // w)),
                  pl.BlockSpec((tt, w), lambda b, t: (row(b, t), COL_AG // w)),
                  pl.BlockSpec((SHORT_CONV, w), lambda b, t: (0, 0)),
                  vec, mat, vec, mat, vec, vec, mat, vec],
        out_specs=pl.BlockSpec((tt, w), lambda b, t: (row(b, t), 0)),
        out_shape=jax.ShapeDtypeStruct((batch * seq, w), BF16),
        scratch_shapes=[pltpu.VMEM((SUBLANE, w), F32), pltpu.VMEM((1, w), F32)],
        compiler_params=_params("parallel", "arbitrary"),
        name="rg_lru",
    )(z, z, p["lru_conv_w"], p["lru_conv_b"], p["lru_wa"], p["lru_ba"], p["lru_wx"],
      p["lru_bx"], p["lru_lambda"], p["gmat64"], p["norm_a"])


def _logf_cumsum_kernel(s_ref, bias_ref, o_ref, carry_ref):
    tt = s_ref.shape[0]

    @pl.when(pl.program_id(1) == 0)
    def _():
        carry_ref[...] = jnp.zeros_like(carry_ref)

    log_f = -_softplus(-(s_ref[...] + bias_ref[...]))
    c = _scan_add(log_f, tt) + carry_ref[...]
    carry_ref[...] = c[tt - 1:tt, :]
    o_ref[...] = c


def logf_cumsum_call(z, bias, batch, seq, tt=1024):
    tt = min(tt, seq)
    nt = seq // tt
    return pl.pallas_call(
        _logf_cumsum_kernel,
        grid=(batch, nt),
        in_specs=[pl.BlockSpec((tt, LANE), lambda b, t: (b * nt + t, COL_SMALL // LANE)),
                  pl.BlockSpec((1, LANE), lambda b, t: (0, 0))],
        out_specs=pl.BlockSpec((tt, LANE), lambda b, t: (b * nt + t, 0)),
        out_shape=jax.ShapeDtypeStruct((batch * seq, LANE), F32),
        scratch_shapes=[pltpu.VMEM((1, LANE), F32)],
        compiler_params=_params("parallel", "arbitrary"),
        name="logf_cumsum",
    )(z, bias)


FOX_ROW_CHUNK = 64


def _fox_kernel(q_ref, k_ref, v_ref, cq_ref, ck_ref, gain_ref, o_ref, kb_ref, vb_ref,
                s_s, p_s, m_s, alpha_s, cq_s, acc_s, *, tq, tk):
    qi = pl.program_id(2)

    @pl.when(qi == 0)
    def _():
        kb_ref[...] = k_ref[...].astype(BF16)
        vb_ref[:, :HEAD_DIM] = v_ref[...].astype(BF16)
        vb_ref[:, HEAD_DIM:] = jnp.ones((vb_ref.shape[0], HEAD_DIM), BF16)

    q = (q_ref[...] * (ATTN_SCALE * LOG2E)).astype(BF16)
    cq_s[...] = jnp.broadcast_to(cq_ref[...] * LOG2E, (tq, LANE))
    m_s[...] = jnp.full((tq, LANE), NEG_INF, F32)
    acc_s[...] = jnp.zeros((tq, 2 * HEAD_DIM), F32)
    rc = min(FOX_ROW_CHUNK, tq)

    def lanes(x, n):
        return jnp.concatenate([x] * (n // LANE), axis=1)

    def scores(kv, slot):
        start = pl.multiple_of(kv * tk, tk)
        s_s[slot] = _dot_nt(q, kb_ref[pl.ds(start, tk), :]) - ck_ref[kv] * LOG2E

    def update(kv, slot, diagonal):
        for r0 in range(0, tq, rc):
            rows = slice(r0, r0 + rc)
            s = s_s[slot, rows, :]
            if diagonal:
                row = lax.broadcasted_iota(jnp.int32, s.shape, 0) + r0
                col = lax.broadcasted_iota(jnp.int32, s.shape, 1)
                s = jnp.where(col <= row, s, NEG_INF)
            m_old = m_s[rows, :]
            m_new = jnp.maximum(m_old, jnp.max(s, axis=-1, keepdims=True) + cq_s[rows, :])
            p_s[rows, :] = jnp.exp2(s - lanes(m_new - cq_s[rows, :], tk)).astype(BF16)
            alpha_s[rows, :] = jnp.exp2(m_old - m_new)
            m_s[rows, :] = m_new
        start = pl.multiple_of(kv * tk, tk)
        pv = _dot(p_s[...], vb_ref[pl.ds(start, tk), :])
        acc_s[...] = lanes(alpha_s[...], 2 * HEAD_DIM) * acc_s[...] + pv

    def pair(jj, carry):
        scores(2 * jj + 1, 1)
        update(2 * jj, 0, False)
        scores(2 * jj + 2, 0)
        update(2 * jj + 1, 1, False)
        return carry

    scores(0, 0)
    lax.fori_loop(0, qi // 2, pair, 0)

    @pl.when(qi % 2 == 1)
    def _():
        scores(qi, 1)
        update(qi - 1, 0, False)
        update(qi, 1, True)

    @pl.when(qi % 2 == 0)
    def _():
        update(qi, 0, True)

    acc = acc_s[...]
    o = acc[:, :HEAD_DIM] / acc[:, HEAD_DIM:]
    o = o * lax.rsqrt(jnp.mean(o * o, axis=-1, keepdims=True) + EPS) * gain_ref[...]
    o_ref[...] = o.astype(o_ref.dtype)


def fox_call(z3, c, gain, batch, seq, tq=512):
    tq = min(tq, seq)
    tk = tq
    nq = seq // tq
    nk = seq // tk
    qb, kb, vb = (COL_BQKV // LANE, (COL_BQKV + GROUP_WIDTH) // LANE,
                  (COL_BQKV + 2 * GROUP_WIDTH) // LANE)
    c_col = c.reshape(batch, N_HEADS, seq, 1)
    c_row = c.reshape(batch, N_HEADS, nk, 1, tk)
    return pl.pallas_call(
        functools.partial(_fox_kernel, tq=tq, tk=tk),
        grid=(batch, N_HEADS, nq),
        in_specs=[pl.BlockSpec((None, tq, LANE), lambda b, h, i: (b, i, qb + h)),
                  pl.BlockSpec((None, seq, LANE), lambda b, h, i: (b, 0, kb + h)),
                  pl.BlockSpec((None, seq, LANE), lambda b, h, i: (b, 0, vb + h)),
                  pl.BlockSpec((None, None, tq, 1), lambda b, h, i: (b, h, i, 0)),
                  pl.BlockSpec((None, None, nk, 1, tk), lambda b, h, i: (b, h, 0, 0, 0)),
                  pl.BlockSpec((1, LANE), lambda b, h, i: (0, h))],
        out_specs=pl.BlockSpec((None, tq, LANE), lambda b, h, i: (b, i, h)),
        out_shape=jax.ShapeDtypeStruct((batch, seq, GROUP_WIDTH), BF16),
        scratch_shapes=[pltpu.VMEM((seq, LANE), BF16), pltpu.VMEM((seq, 2 * LANE), BF16),
                        pltpu.VMEM((2, tq, tk), F32), pltpu.VMEM((tq, tk), BF16),
                        pltpu.VMEM((tq, LANE), F32), pltpu.VMEM((tq, LANE), F32),
                        pltpu.VMEM((tq, LANE), F32), pltpu.VMEM((tq, 2 * HEAD_DIM), F32)],
        compiler_params=_params("parallel", "parallel", "arbitrary"),
        name="fox_attention",
    )(z3, z3, z3, c_col, c_row, gain)


def _l2norm(t):
    return t * lax.rsqrt(jnp.sum(t * t, axis=-1, keepdims=True) + EPS)


def _bmm(a, b):
    return lax.dot_general(a, b, (((2,), (1,)), ((0,), (0,))), preferred_element_type=F32)


def _bmm_nt(a, b):
    return lax.dot_general(a, b, (((2,), (2,)), ((0,), (0,))), preferred_element_type=F32)


def _bmm_tn(a, b):
    return lax.dot_general(a, b, (((1,), (1,)), ((0,), (0,))), preferred_element_type=F32)


def _neumann_tail(p):
    c = p.shape[-1]
    a = p
    q = p
    cover = 2
    while cover < c:
        qb = q.astype(BF16)
        q = _bmm(qb, qb)
        a = a + q + _bmm(a.astype(BF16), q.astype(BF16))
        cover *= 2
    return a


def _gdn_kernel(x_ref, zg_ref, s_ref, cw_ref, aneg_ref, dt_ref, gain_ref, fbias_ref,
                o_ref, cf_ref, halo_ref, qkv_ref, state_ref, fcarry_ref):
    tt = x_ref.shape[0]
    c = GDN_CHUNK
    nc = tt // c
    w = GROUP_WIDTH
    hd = HEAD_DIM

    @pl.when(pl.program_id(1) == 0)
    def _():
        halo_ref[...] = jnp.zeros_like(halo_ref)
        state_ref[...] = jnp.zeros_like(state_ref)
        fcarry_ref[...] = jnp.zeros_like(fcarry_ref)

    x = x_ref[...]
    qkv_ref[...] = _silu(_causal_conv(x, halo_ref[...], cw_ref[...], SHORT_CONV))
    halo_ref[...] = x[tt - SUBLANE:, :]
    small = s_ref[...]
    cf = _scan_add(-_softplus(-(small + fbias_ref[...])), tt) + fcarry_ref[...]
    fcarry_ref[...] = cf[tt - 1:tt, :]
    cf_ref[...] = cf
    beta_all = _sigmoid(small)
    gc_all = _scan_add(aneg_ref[...] * _softplus(small + dt_ref[...]), c)

    pairs = [(ci, h) for ci in range(nc) for h in range(N_HEADS)]

    def gather(col0):
        return jnp.stack([qkv_ref[ci * c:(ci + 1) * c, col0 + h * hd:col0 + (h + 1) * hd]
                          for ci, h in pairs])

    q = _l2norm(gather(0)) * ATTN_SCALE
    k = _l2norm(gather(w))
    v = gather(2 * w)
    beta = jnp.stack([beta_all[ci * c:(ci + 1) * c, SMALL_BETA + h:SMALL_BETA + h + 1]
                      for ci, h in pairs])
    gc = jnp.stack([gc_all[ci * c:(ci + 1) * c, SMALL_ALPHA + h:SMALL_ALPHA + h + 1]
                    for ci, h in pairs])
    gc_t = [gc_all[ci * c:(ci + 1) * c, :].T for ci in range(nc)]
    gc_row = jnp.stack([gc_t[ci][SMALL_ALPHA + h:SMALL_ALPHA + h + 1, :]
                        for ci, h in pairs])

    row = lax.broadcasted_iota(jnp.int32, (c, c), 0)
    col = lax.broadcasted_iota(jnp.int32, (c, c), 1)
    tril = col <= row
    strict = col < row
    decay = jnp.where(tril, jnp.exp(jnp.where(tril, gc - gc_row, 0.0)), 0.0)
    kb = k.astype(BF16)
    kbeta = k * beta
    kk = _bmm_nt(kbeta.astype(BF16), kb) * decay
    tail = _neumann_tail(jnp.where(strict, -kk, 0.0))
    rhs = jnp.concatenate([v * beta, kbeta * jnp.exp(gc)], axis=-1)
    sol = rhs + _bmm(tail.astype(BF16), rhs.astype(BF16))
    u = sol[:, :, :hd]
    wy = sol[:, :, hd:]
    qk = jnp.where(tril, _bmm_nt(q.astype(BF16), kb) * decay, 0.0).astype(BF16)
    g_last = gc[:, c - 1:c, :]
    k_dec = (k * jnp.exp(g_last - gc)).astype(BF16)
    e_last = jnp.exp(g_last)
    wq = jnp.concatenate([wy.astype(BF16), (q * jnp.exp(gc)).astype(BF16)], axis=1)

    state = state_ref[...]
    for ci in range(nc):
        sl = slice(ci * N_HEADS, (ci + 1) * N_HEADS)
        ws = _bmm(wq[sl], state.astype(BF16))
        v_new = (u[sl] - ws[:, :c, :]).astype(BF16)
        o = ws[:, c:, :] + _bmm(qk[sl], v_new)
        state = state * e_last[sl] + _bmm_tn(k_dec[sl], v_new)
        o = o * lax.rsqrt(jnp.mean(o * o, axis=-1, keepdims=True) + EPS) * gain_ref[...]
        for h in range(N_HEADS):
            zg = zg_ref[ci * c:(ci + 1) * c, h * hd:(h + 1) * hd]
            o_ref[ci * c:(ci + 1) * c, h * hd:(h + 1) * hd] = (o[h] * _silu(zg)).astype(o_ref.dtype)
    state_ref[...] = state


def gdn_call(z, p, batch, seq, tt=512):
    tt = min(tt, seq)
    nt = seq // tt
    w = GROUP_WIDTH
    vec = pl.BlockSpec((1, LANE), lambda b, t: (0, 0))
    return pl.pallas_call(
        _gdn_kernel,
        grid=(batch, nt),
        in_specs=[pl.BlockSpec((tt, 3 * w), lambda b, t: (b * nt + t, COL_CQKV // (3 * w))),
                  pl.BlockSpec((tt, w), lambda b, t: (b * nt + t, COL_CZ // w)),
                  pl.BlockSpec((tt, LANE), lambda b, t: (b * nt + t, COL_SMALL // LANE)),
                  pl.BlockSpec((SHORT_CONV, 3 * w), lambda b, t: (0, 0)),
                  vec, vec, vec, vec],
        out_specs=[pl.BlockSpec((tt, w), lambda b, t: (b * nt + t, 0)),
                   pl.BlockSpec((tt, LANE), lambda b, t: (b * nt + t, 0))],
        out_shape=[jax.ShapeDtypeStruct((batch * seq, w), BF16),
                   jax.ShapeDtypeStruct((batch * seq, LANE), F32)],
        scratch_shapes=[pltpu.VMEM((SUBLANE, 3 * w), F32),
                        pltpu.VMEM((tt, 3 * w), F32),
                        pltpu.VMEM((N_HEADS, HEAD_DIM, HEAD_DIM), F32),
                        pltpu.VMEM((1, LANE), F32)],
        compiler_params=_params("parallel", "arbitrary"),
        name="gated_deltanet",
    )(z, z, z, p["gdn_conv_w"], p["gdn_aneg"], p["gdn_dt"], p["gdn_norm"], p["fox_bias"])


def _dilated_kernel(q_ref, kc_ref, kp_ref, vc_ref, vp_ref, gain_ref, o_ref,
                    m_ref, acc_ref, l_ref):
    tt = q_ref.shape[0]
    first = pl.program_id(2) == 0
    i = lax.broadcasted_iota(jnp.int32, (SPAN, 2 * SPAN), 0)
    j = lax.broadcasted_iota(jnp.int32, (SPAN, 2 * SPAN), 1)
    band = jnp.logical_and(j >= i, j <= i + SPAN)
    band_first = jnp.logical_and(band, jnp.logical_or(j >= SPAN, jnp.logical_not(first)))
    ones = jnp.ones((2 * SPAN, HEAD_DIM), BF16)

    def rows(ref, start, n, dil):
        return ref[pl.ds(start, n, stride=dil), :] if dil > 1 else ref[pl.ds(start, n), :]

    for g, dil in enumerate(DILATIONS):
        for r in range(dil):
            for sb in range(tt // (dil * SPAN)):
                q0 = r + dil * SPAN * sb
                q = (rows(q_ref, q0, SPAN, dil) * (ATTN_SCALE * LOG2E)).astype(BF16)
                if sb == 0:
                    p0 = tt - dil * SPAN + r
                    k = jnp.concatenate([rows(kp_ref, p0, SPAN, dil), rows(kc_ref, r, SPAN, dil)], 0)
                    v = jnp.concatenate([rows(vp_ref, p0, SPAN, dil), rows(vc_ref, r, SPAN, dil)], 0)
                else:
                    k = rows(kc_ref, q0 - dil * SPAN, 2 * SPAN, dil)
                    v = rows(vc_ref, q0 - dil * SPAN, 2 * SPAN, dil)
                s = _dot_nt(q, k.astype(BF16))
                s = jnp.where(band_first if sb == 0 else band, s, NEG_INF)
                v1 = jnp.concatenate([v.astype(BF16), ones], axis=1)
                m_blk = jnp.broadcast_to(jnp.max(s, axis=-1, keepdims=True), (SPAN, LANE))
                if g == 0:
                    m_new = m_blk
                else:
                    m_old = rows(m_ref, q0, SPAN, dil)
                    m_new = jnp.maximum(m_old, m_blk)
                p = jnp.exp2(s - jnp.concatenate([m_new, m_new], axis=1))
                pv = _dot(p.astype(BF16), v1)
                acc, l = pv[:, :HEAD_DIM], pv[:, HEAD_DIM:]
                if g > 0:
                    a = jnp.exp2(m_old - m_new)
                    acc = acc + a * rows(acc_ref, q0, SPAN, dil)
                    l = l + a * rows(l_ref, q0, SPAN, dil)
                dst = pl.ds(q0, SPAN, stride=dil) if dil > 1 else pl.ds(q0, SPAN)
                m_ref[dst, :] = m_new
                acc_ref[dst, :] = acc
                l_ref[dst, :] = l
    y = acc_ref[...] / l_ref[...]
    y = y * lax.rsqrt(jnp.mean(y * y, axis=-1, keepdims=True) + EPS) * gain_ref[...]
    o_ref[...] = y.astype(o_ref.dtype)


def dilated_call(z3, gain, batch, seq):
    tt = max(DILATIONS) * SPAN
    assert seq % tt == 0
    nt = seq // tt
    qb, kb, vb = (COL_DQKV // LANE, (COL_DQKV + GROUP_WIDTH) // LANE,
                  (COL_DQKV + 2 * GROUP_WIDTH) // LANE)

    def cur(off):
        return pl.BlockSpec((None, tt, LANE), lambda b, h, i: (b, i, off + h))

    def prev(off):
        return pl.BlockSpec((None, tt, LANE), lambda b, h, i: (b, jnp.maximum(i - 1, 0), off + h))

    return pl.pallas_call(
        _dilated_kernel,
        grid=(batch, N_HEADS, nt),
        in_specs=[cur(qb), cur(kb), prev(kb), cur(vb), prev(vb),
                  pl.BlockSpec((1, LANE), lambda b, h, i: (0, h))],
        out_specs=pl.BlockSpec((None, tt, LANE), lambda b, h, i: (b, i, h)),
        out_shape=jax.ShapeDtypeStruct((batch, seq, GROUP_WIDTH), BF16),
        scratch_shapes=[pltpu.VMEM((tt, LANE), F32), pltpu.VMEM((tt, HEAD_DIM), F32),
                        pltpu.VMEM((tt, LANE), F32)],
        compiler_params=_params("parallel", "parallel", "arbitrary"),
        name="dilated_attention",
    )(z3, z3, z3, z3, z3, gain)


def _out_proj_kernel(ya, yb, yc, yd, w_ref, x_ref, g_ref, x1_ref, h_ref):
    w = GROUP_WIDTH
    acc = x_ref[...]
    for n, y in enumerate((ya, yb, yc, yd)):
        acc = acc + _dot(y[...], w_ref[n * w:(n + 1) * w, :])
    x1_ref[...] = acc
    h_ref[...] = _rmsnorm_rows(acc, g_ref[...]).astype(h_ref.dtype)


def out_proj_call(ys, w_out, layer, x, gain, tm=512):
    m, d = x.shape
    tm = min(tm, m)
    yblk = pl.BlockSpec((tm, GROUP_WIDTH), lambda i: (i, 0))
    xblk = pl.BlockSpec((tm, d), lambda i: (i, 0))
    return pl.pallas_call(
        _out_proj_kernel,
        grid=(m // tm,),
        in_specs=[yblk] * 4 + [pl.BlockSpec((None, d, d), lambda i: (layer, 0, 0),
                                            pipeline_mode=pl.Buffered(1)), xblk,
                               pl.BlockSpec((1, d), lambda i: (0, 0))],
        out_specs=[xblk, xblk],
        out_shape=[jax.ShapeDtypeStruct((m, d), F32), jax.ShapeDtypeStruct((m, d), BF16)],
        compiler_params=_params("parallel"),
        name="out_proj",
    )(*ys, w_out, x, gain)


FFN_SUB_ROWS = 512


def _ffn_up_kernel(h_ref, wu32_ref, wg32_ref, cwu_ref, cwg_ref, cbu_ref, cbg_ref, o_ref,
                   tail_u_ref, tail_g_ref, wu_ref, wg_ref, *, tiles_per_seq):
    tm = h_ref.shape[0]

    @pl.when(pl.program_id(1) == 0)
    def _():
        wu_ref[...] = wu32_ref[...].astype(BF16)
        wg_ref[...] = wg32_ref[...].astype(BF16)

    @pl.when(pl.program_id(1) % tiles_per_seq == 0)
    def _():
        tail_u_ref[...] = jnp.zeros_like(tail_u_ref)
        tail_g_ref[...] = jnp.zeros_like(tail_g_ref)

    sub = min(FFN_SUB_ROWS, tm)
    tail_u = tail_u_ref[...]
    tail_g = tail_g_ref[...]
    for r0 in range(0, tm, sub):
        h = h_ref[r0:r0 + sub, :]
        u = _dot(h, wu_ref[...])
        g = _dot(h, wg_ref[...])
        up = _causal_conv(u, tail_u, cwu_ref[...], FFN_CONV) + cbu_ref[...]
        gate = _causal_conv(g, tail_g, cwg_ref[...], FFN_CONV) + cbg_ref[...]
        tail_u = u[sub - SUBLANE:, :]
        tail_g = g[sub - SUBLANE:, :]
        o_ref[r0:r0 + sub, :] = (_silu(gate) * up).astype(o_ref.dtype)
    tail_u_ref[...] = tail_u
    tail_g_ref[...] = tail_g


def ffn_up_call(h, w_up, layer, conv_w, conv_b, seq, tm=1024, tn=512):
    m, d = h.shape
    tm = min(tm, seq)
    nj = D_FF // tn
    return pl.pallas_call(
        functools.partial(_ffn_up_kernel, tiles_per_seq=seq // tm),
        grid=(nj, m // tm),
        in_specs=[pl.BlockSpec((tm, d), lambda j, i: (i, 0)),
                  pl.BlockSpec((None, d, tn), lambda j, i: (layer, 0, j)),
                  pl.BlockSpec((None, d, tn), lambda j, i: (layer, 0, j + nj)),
                  pl.BlockSpec((FFN_CONV, tn), lambda j, i: (0, j)),
                  pl.BlockSpec((FFN_CONV, tn), lambda j, i: (0, j + nj)),
                  pl.BlockSpec((1, tn), lambda j, i: (0, j)),
                  pl.BlockSpec((1, tn), lambda j, i: (0, j + nj))],
        out_specs=pl.BlockSpec((tm, tn), lambda j, i: (i, j)),
        out_shape=jax.ShapeDtypeStruct((m, D_FF), BF16),
        scratch_shapes=[pltpu.VMEM((SUBLANE, tn), F32), pltpu.VMEM((SUBLANE, tn), F32),
                        pltpu.VMEM((d, tn), BF16), pltpu.VMEM((d, tn), BF16)],
        compiler_params=_params("parallel", "arbitrary"),
        name="ffn_up",
    )(h, w_up, w_up, conv_w, conv_w, conv_b, conv_b)


def _ffn_down_kernel(g_ref, w_ref, x_ref, gain_ref, x2_ref, hn_ref):
    x2 = x_ref[...] + _dot(g_ref[...], w_ref[...])
    x2_ref[...] = x2
    hn_ref[...] = _rmsnorm_rows(x2, gain_ref[...]).astype(hn_ref.dtype)


def ffn_down_call(g, w_down, layer, x, gain, hn_dtype, tm=256):
    m, d = x.shape
    tm = min(tm, m)
    kdim = g.shape[1]
    xblk = pl.BlockSpec((tm, d), lambda i: (i, 0))
    return pl.pallas_call(
        _ffn_down_kernel,
        grid=(m // tm,),
        in_specs=[pl.BlockSpec((tm, kdim), lambda i: (i, 0)),
                  pl.BlockSpec((None, kdim, d), lambda i: (layer, 0, 0),
                               pipeline_mode=pl.Buffered(1)),
                  xblk, pl.BlockSpec((1, d), lambda i: (0, 0))],
        out_specs=[xblk, xblk],
        out_shape=[jax.ShapeDtypeStruct((m, d), F32), jax.ShapeDtypeStruct((m, d), hn_dtype)],
        compiler_params=_params("parallel"),
        name="ffn_down",
    )(g, w_down, x, gain)


def _block_diag(blocks):
    n, r, c = blocks.shape
    eye = jnp.eye(n, dtype=blocks.dtype)
    return (eye[:, None, :, None] * blocks[:, :, None, :]).reshape(n * r, n * c)


def _small_vec(values, offset):
    return jnp.zeros((1, LANE), F32).at[0, offset:offset + values.shape[0]].set(values)


def _w_in_segments():
    gw = GROUP_WIDTH
    sizes = (gw, gw, 3 * gw, N_HEADS, 3 * gw, gw, N_HEADS, N_HEADS, 3 * gw)
    dsts = (COL_AX, COL_AG, COL_BQKV, COL_SMALL + SMALL_F, COL_CQKV, COL_CZ,
            COL_SMALL + SMALL_BETA, COL_SMALL + SMALL_ALPHA, COL_DQKV)
    segs, src = [], 0
    for size, dst in zip(sizes, dsts):
        segs.append((src, dst, size))
        src += size
    return segs, src


def _prep_w_in_kernel(w_ref, o_ref):
    segs, _ = _w_in_segments()
    o_ref[:, COL_SMALL:] = jnp.zeros((o_ref.shape[0], Z_COLS - COL_SMALL), o_ref.dtype)
    for src, dst, size in segs:
        lo = (src // LANE) * LANE
        hi = min(-(-(src + size) // LANE) * LANE, w_ref.shape[1])
        tile = w_ref[:, lo:hi]
        o_ref[:, dst:dst + size] = tile[:, src - lo:src - lo + size].astype(o_ref.dtype)


def prep_w_in_call(w_in, tk=256):
    depth, k, n = w_in.shape
    assert n == _w_in_segments()[1]
    return pl.pallas_call(
        _prep_w_in_kernel,
        grid=(depth, k // tk),
        in_specs=[pl.BlockSpec((None, tk, n), lambda l, i: (l, i, 0))],
        out_specs=pl.BlockSpec((None, tk, Z_COLS), lambda l, i: (l, i, 0)),
        out_shape=jax.ShapeDtypeStruct((depth, k, Z_COLS), BF16),
        compiler_params=_params("parallel", "parallel"),
        name="prep_w_in",
    )(w_in)


def kernel(x, norm_mix, w_in, lru_conv_w, lru_conv_b, lru_wa, lru_ba, lru_wx, lru_bx,
           lru_lambda, fox_f_bias, gdn_conv_w, gdn_a_log, gdn_dt_bias, gdn_norm,
           norm_a, norm_b, norm_d, w_out, norm_ffn, ffn_w_up, ffn_conv_w, ffn_conv_b,
           ffn_w_down, norm_final):
    batch, seq, d = x.shape
    depth = w_in.shape[0]
    m = batch * seq
    gw = GROUP_WIDTH
    gmat64 = _block_diag(jnp.full((LRU_BLOCKS, LRU_BLOCK_DIM, LRU_BLOCK_DIM),
                                  1.0 / LRU_BLOCK_DIM, F32)).astype(BF16)
    xs = x.reshape(m, d)
    w_in_z = prep_w_in_call(w_in)
    w_out_b = w_out.astype(BF16)
    w_down_b = ffn_w_down.astype(BF16)
    h = rmsnorm_call(xs, norm_mix[0], BF16)
    for l in range(depth):
        p = {
            "lru_conv_w": lru_conv_w[l], "lru_conv_b": lru_conv_b[l].reshape(1, gw),
            "lru_wa": _block_diag(lru_wa[l]).astype(BF16), "lru_ba": lru_ba[l].reshape(1, gw),
            "lru_wx": _block_diag(lru_wx[l]).astype(BF16), "lru_bx": lru_bx[l].reshape(1, gw),
            "lru_lambda": lru_lambda[l].reshape(1, gw), "gmat64": gmat64,
            "norm_a": norm_a[l].reshape(1, gw),
            "gdn_conv_w": gdn_conv_w[l],
            "gdn_aneg": _small_vec(-jnp.exp(gdn_a_log[l]), SMALL_ALPHA),
            "gdn_dt": _small_vec(gdn_dt_bias[l], SMALL_ALPHA),
            "gdn_norm": gdn_norm[l].reshape(1, HEAD_DIM),
            "fox_bias": _small_vec(fox_f_bias[l], SMALL_F),
        }
        z = in_proj_call(h, w_in_z, l)

        y_a = lru_call(z, p, batch, seq)

        y_c, c = gdn_call(z, p, batch, seq)

        c = c.reshape(batch, seq, LANE)[:, :, SMALL_F:SMALL_F + N_HEADS]
        c = jnp.transpose(c, (0, 2, 1))
        z3 = z.reshape(batch, seq, Z_COLS)
        y_b = fox_call(z3, c, norm_b[l].reshape(1, gw), batch, seq).reshape(m, gw)

        y_d = dilated_call(z3, norm_d[l].reshape(1, gw), batch, seq).reshape(m, gw)

        x1, hf = out_proj_call((y_a, y_b, y_c, y_d), w_out_b, l, xs,
                               norm_ffn[l].reshape(1, d))
        g = ffn_up_call(hf, ffn_w_up, l, ffn_conv_w[l], ffn_conv_b[l].reshape(1, 2 * D_FF), seq)
        last = l == depth - 1
        gain_next = norm_final if last else norm_mix[l + 1]
        xs, h = ffn_down_call(g, w_down_b, l, x1, gain_next.reshape(1, d),
                              F32 if last else BF16)
    return h.reshape(batch, seq, d)
```
